```python
import math
import jax, jax.numpy as jnp
from jax import lax
import numpy as np

D_MODEL = 2048
BATCH = 4
SEQ = 2048
DEPTH = 1
DEC_BATCH = 32
DEC_SEQ = 8
PAST_LEN = 8192
PAGE_SIZE = 128

H_RET = 8
DK_RET = 256
DV_RET = 256
RET_CHUNK = 128
H_ATT = 8
HD_ATT = 128
MOBA_BLOCK = 256
MOBA_TOPK = 3
MOBA_QROWS = 64
D_FF = 4 * D_MODEL
RET_QK = H_RET * DK_RET
RET_V = H_RET * DV_RET
ATT_W = H_ATT * HD_ATT
W_IN_COLS = 2 * RET_QK + 2 * RET_V + 3 * ATT_W + 2 * D_MODEL
ROPE_BASE = 10000.0
LN_EPS = 1e-5
GN_EPS = 1e-5
ALPHA = (2 * DEPTH) ** 0.25
BETA = (8 * DEPTH) ** -0.25

kernel_name = 'retnet_moba_gated_hybrid_step'


def _layer_norm(x, g, b):
    x32 = x.astype(jnp.float32)
    mu = jnp.mean(x32, axis=-1, keepdims=True)
    var = jnp.mean(jnp.square(x32 - mu), axis=-1, keepdims=True)
    return ((x32 - mu) * lax.rsqrt(var + LN_EPS) * g + b).astype(x.dtype)


def _rotary(x, pos):
    half = x.shape[-1] // 2
    inv = ROPE_BASE ** (-jnp.arange(half, dtype=jnp.float32) / half)
    ang = pos.astype(jnp.float32)[:, None] * inv[None, :]
    cos = jnp.cos(ang)[None, :, None, :]
    sin = jnp.sin(ang)[None, :, None, :]
    x32 = x.astype(jnp.float32)
    x1, x2 = x32[..., :half], x32[..., half:]
    return jnp.concatenate([x1 * cos - x2 * sin, x1 * sin + x2 * cos], axis=-1).astype(x.dtype)


def _log_gamma():
    return jnp.log1p(-jnp.exp2(-5.0 - jnp.arange(H_RET, dtype=jnp.float32)))


def _mixer_in(x, pos, w_in):
    B, T = x.shape[0], x.shape[1]
    z = jnp.einsum('btd,de->bte', x, w_in)
    cuts = [int(c) for c in np.cumsum([RET_QK, RET_QK, RET_V, RET_V, ATT_W, ATT_W, ATT_W, D_MODEL])]
    q_r, k_r, v_r, g_r, q_a, k_a, v_a, gb_r, gb_a = jnp.split(z, cuts, axis=-1)
    q_r = _rotary(q_r.reshape(B, T, H_RET, DK_RET), pos)
    k_r = _rotary(k_r.reshape(B, T, H_RET, DK_RET), pos) * (DK_RET ** -0.5)
    v_r = v_r.reshape(B, T, H_RET, DV_RET)
    q_a = q_a.reshape(B, T, H_ATT, HD_ATT)
    k_a = k_a.reshape(B, T, H_ATT, HD_ATT)
    v_a = v_a.reshape(B, T, H_ATT, HD_ATT)
    return q_r, k_r, v_r, g_r, q_a, k_a, v_a, gb_r, gb_a


def _retention_chunk(q, k, v, S, log_gamma):
    f32 = jnp.float32
    q, k, v, S = (a.astype(f32) for a in (q, k, v, S))
    C = q.shape[1]
    idx = jnp.arange(C, dtype=f32)
    diff = idx[:, None] - idx[None, :]
    causal = diff >= 0
    decay = jnp.where(causal, jnp.exp(log_gamma[:, None, None] * jnp.where(causal, diff, 0.0)), 0.0)
    scores = jnp.einsum('bihd,bjhd->bhij', q, k) * decay
    inner = jnp.einsum('bhij,bjhe->bihe', scores, v)
    q_decay = jnp.exp(log_gamma[None, :] * (idx[:, None] + 1.0))
    cross = jnp.einsum('bihd,bhde->bihe', q, S) * q_decay[None, :, :, None]
    k_decay = jnp.exp(log_gamma[None, :] * (C - 1.0 - idx[:, None]))
    S_new = (jnp.exp(log_gamma * C)[None, :, None, None] * S
             + jnp.einsum('bjhd,bjhe->bhde', k * k_decay[None, :, :, None], v))
    return inner + cross, S_new


def _retention_prompt(q, k, v, log_gamma):
    B, T = q.shape[0], q.shape[1]
    n = T // RET_CHUNK

    def to_chunks(a):
        return a.reshape(B, n, RET_CHUNK, a.shape[2], a.shape[3]).swapaxes(0, 1)

    def step(S, qkv):
        o, S_new = _retention_chunk(qkv[0], qkv[1], qkv[2], S, log_gamma)
        return S_new, o

    S0 = jnp.zeros((B, H_RET, DK_RET, DV_RET), jnp.float32)
    S, o = lax.scan(step, S0, (to_chunks(q), to_chunks(k), to_chunks(v)))
    return o.swapaxes(0, 1).reshape(B, T, H_RET, DV_RET), S


def _pad_blocks(a):
    pad = -a.shape[1] % MOBA_BLOCK
    return jnp.pad(a, ((0, 0), (0, pad), (0, 0), (0, 0)))


def _moba(q, k, v, q_pos):
    B, Tq = q.shape[0], q.shape[1]
    nb = k.shape[1] // MOBA_BLOCK
    n_sel = min(MOBA_TOPK, nb)
    kb = k.reshape(B, nb, MOBA_BLOCK, H_ATT, HD_ATT)
    vb = v.reshape(B, nb, MOBA_BLOCK, H_ATT, HD_ATT)
    k_mean = jnp.mean(kb, axis=2, dtype=jnp.float32)
    bi = jnp.arange(B)[:, None, None, None]
    hi = jnp.arange(H_ATT)[None, :, None, None]
    offs = jnp.arange(MOBA_BLOCK, dtype=jnp.int32)
    is_sel = jnp.arange(n_sel + 1) < n_sel
    scale = HD_ATT ** -0.5

    def attend(args):
        qc, pc = args
        qblk = pc // MOBA_BLOCK
        s = jnp.einsum('bqhd,bnhd->bhqn', qc, k_mean, preferred_element_type=jnp.float32)
        eligible = jnp.arange(nb)[None, :] < qblk[:, None]
        s = jnp.where(eligible, s, -jnp.inf)
        _, sel = lax.top_k(s, n_sel)
        own = jnp.broadcast_to(qblk[None, None, :, None], sel.shape[:3] + (1,)).astype(sel.dtype)
        blocks = jnp.concatenate([sel, own], axis=-1)
        kg = kb[bi, blocks, :, hi]
        vg = vb[bi, blocks, :, hi]
        kpos = blocks[..., None] * MOBA_BLOCK + offs
        blk_ok = jnp.logical_or(~is_sel, blocks < qblk[:, None])
        valid = blk_ok[..., None] & (kpos <= pc[:, None, None])
        logits = jnp.einsum('bqhd,bhqjkd->bhqjk', qc, kg, preferred_element_type=jnp.float32) * scale
        logits = jnp.where(valid, logits, -jnp.inf)
        p = jax.nn.softmax(logits.reshape(logits.shape[0], logits.shape[1], logits.shape[2], -1), axis=-1)
        p = p.reshape(logits.shape).astype(vg.dtype)
        return jnp.einsum('bhqjk,bhqjkd->bqhd', p, vg)

    qb = math.gcd(Tq, max(1, MOBA_QROWS // B))
    n = Tq // qb
    qs = q.reshape(B, n, qb, H_ATT, HD_ATT).swapaxes(0, 1)
    ps = q_pos.reshape(n, qb)
    out = lax.map(attend, (qs, ps))
    return out.swapaxes(0, 1).reshape(B, Tq, H_ATT, HD_ATT)


def _merge(x, o_r, g_r, o_a, gb_r, gb_a, gn_gain, w_ret_br, w_att_br, w_out,
           ln1_g, ln1_b, w_up, w_down, ln2_g, ln2_b):
    B, T = x.shape[0], x.shape[1]
    o32 = o_r.astype(jnp.float32)
    mu = jnp.mean(o32, axis=-1, keepdims=True)
    var = jnp.mean(jnp.square(o32 - mu), axis=-1, keepdims=True)
    o_n = ((o32 - mu) * lax.rsqrt(var + GN_EPS) * gn_gain).reshape(B, T, RET_V).astype(x.dtype)
    ret = jnp.einsum('bte,ed->btd', o_n * jax.nn.silu(g_r), w_ret_br)
    att = jnp.einsum('bte,ed->btd', o_a.reshape(B, T, ATT_W), w_att_br)
    mix = jnp.einsum('btd,de->bte', jax.nn.sigmoid(gb_r) * ret + jax.nn.sigmoid(gb_a) * att, w_out)
    h = _layer_norm(ALPHA * x + mix, ln1_g, ln1_b)
    ff = jnp.einsum('btf,fd->btd', jnp.square(jax.nn.relu(jnp.einsum('btd,df->btf', h, w_up))), w_down)
    return _layer_norm(ALPHA * h + ff, ln2_g, ln2_b)


def setup_inputs(seed: int = 0) -> dict:
    key = jax.random.key(seed)
    ks = jax.random.split(key, 17)
    f32 = jnp.float32
    n_pages = PAST_LEN // PAGE_SIZE
    n_used = DEC_BATCH * n_pages
    n_phys = n_used + (n_used + 3) // 4

    def nrm(k, shape, s):
        return jax.random.normal(k, shape, f32) * s

    gam = 1.0 - jnp.exp2(-5.0 - jnp.arange(H_RET, dtype=f32))
    st_scale = lax.rsqrt(1.0 - gam * gam) * (DK_RET ** -0.5)
    page_table = jax.random.permutation(ks[5], n_phys)[:n_used].reshape(DEC_BATCH, n_pages).astype(jnp.int32)
    return {
        'x_prompt': nrm(ks[0], (BATCH, SEQ, D_MODEL), 1.0),
        'x_sample': nrm(ks[1], (DEC_BATCH, DEC_SEQ, D_MODEL), 1.0),
        'cache_k': nrm(ks[2], (DEPTH, n_phys, PAGE_SIZE, H_ATT, HD_ATT), 1.0),
        'cache_v': nrm(ks[3], (DEPTH, n_phys, PAGE_SIZE, H_ATT, HD_ATT), 1.0),
        'state_ret': nrm(ks[4], (DEPTH, DEC_BATCH, H_RET, DK_RET, DV_RET), 1.0) * st_scale[None, None, :, None, None],
        'page_table': page_table,
        'w_in': nrm(ks[6], (DEPTH, D_MODEL, W_IN_COLS), D_MODEL ** -0.5),
        'ret_gn_gain': 1.0 + nrm(ks[7], (DEPTH, H_RET, DV_RET), 0.02),
        'w_ret_br': nrm(ks[8], (DEPTH, RET_V, D_MODEL), BETA * RET_V ** -0.5),
        'w_att_br': nrm(ks[9], (DEPTH, ATT_W, D_MODEL), BETA * ATT_W ** -0.5),
        'w_out': nrm(ks[10], (DEPTH, D_MODEL, D_MODEL), BETA * D_MODEL ** -0.5),
        'ln1_g': 1.0 + nrm(ks[11], (DEPTH, D_MODEL), 0.02),
        'ln1_b': nrm(ks[12], (DEPTH, D_MODEL), 0.02),
        'w_up': nrm(ks[13], (DEPTH, D_MODEL, D_FF), D_MODEL ** -0.5),
        'w_down': nrm(ks[14], (DEPTH, D_FF, D_MODEL), BETA * D_FF ** -0.5),
        'ln2_g': 1.0 + nrm(ks[15], (DEPTH, D_MODEL), 0.02),
        'ln2_b': nrm(ks[16], (DEPTH, D_MODEL), 0.02),
    }


def reference(x_prompt, x_sample, cache_k, cache_v, state_ret, page_table,
              w_in, ret_gn_gain, w_ret_br, w_att_br, w_out, ln1_g, ln1_b,
              w_up, w_down, ln2_g, ln2_b):
    pos_p = jnp.arange(SEQ, dtype=jnp.int32)
    pos_s = PAST_LEN + jnp.arange(DEC_SEQ, dtype=jnp.int32)
    log_gamma = _log_gamma()
    pad_s = -(PAST_LEN + DEC_SEQ) % MOBA_BLOCK
    y_p, y_s = x_prompt, x_sample
    kp, vp, sp, ks, vs, ss = [], [], [], [], [], []
    for l in range(DEPTH):
        out_w = (ret_gn_gain[l], w_ret_br[l], w_att_br[l], w_out[l], ln1_g[l], ln1_b[l],
                 w_up[l], w_down[l], ln2_g[l], ln2_b[l])
        q_r, k_r, v_r, g_r, q_a, k_a, v_a, gb_r, gb_a = _mixer_in(y_p, pos_p, w_in[l])
        o_r, s_new = _retention_prompt(q_r, k_r, v_r, log_gamma)
        o_a = _moba(q_a, _pad_blocks(k_a), _pad_blocks(v_a), pos_p)
        y_p = _merge(y_p, o_r, g_r, o_a, gb_r, gb_a, *out_w)
        kp.append(k_a)
        vp.append(v_a)
        sp.append(s_new)
        q_r, k_r, v_r, g_r, q_a, k_a, v_a, gb_r, gb_a = _mixer_in(y_s, pos_s, w_in[l])
        o_r, s_new = _retention_chunk(q_r, k_r, v_r, state_ret[l], log_gamma)
        k_past = cache_k[l, page_table].reshape(DEC_BATCH, -1, H_ATT, HD_ATT).astype(k_a.dtype)
        v_past = cache_v[l, page_table].reshape(DEC_BATCH, -1, H_ATT, HD_ATT).astype(v_a.dtype)
        zpad = jnp.zeros((DEC_BATCH, pad_s, H_ATT, HD_ATT), k_a.dtype)
        k_all = jnp.concatenate([k_past, k_a, zpad], axis=1)
        v_all = jnp.concatenate([v_past, v_a, zpad], axis=1)
        o_a = _moba(q_a, k_all, v_all, pos_s)
        y_s = _merge(y_s, o_r, g_r, o_a, gb_r, gb_a, *out_w)
        ks.append(k_a)
        vs.append(v_a)
        ss.append(s_new)
    return (y_p, y_s, jnp.stack(kp), jnp.stack(vp), jnp.stack(sp), jnp.stack(ks), jnp.stack(vs), jnp.stack(ss))
```

```python
import functools

import jax
import jax.numpy as jnp
from jax import lax
from jax.experimental import pallas as pl
from jax.experimental.pallas import tpu as pltpu

F32 = jnp.float32
BF16 = jnp.bfloat16

D_MODEL = 2048
DEPTH = 1
PAST_LEN = 8192
PAGE_SIZE = 128
H_RET = 8
DK_RET = 256
DV_RET = 256
RET_CHUNK = 128
H_ATT = 8
HD_ATT = 128
MOBA_BLOCK = 256
MOBA_TOPK = 3
D_FF = 4 * D_MODEL
RET_QK = H_RET * DK_RET
RET_V = H_RET * DV_RET
ATT_W = H_ATT * HD_ATT
W_IN_COLS = 2 * RET_QK + 2 * RET_V + 3 * ATT_W + 2 * D_MODEL
ROPE_BASE = 10000.0
LN_EPS = 1e-5
GN_EPS = 1e-5
ALPHA = (2 * DEPTH) ** 0.25

COL_QR = 0
COL_KR = RET_QK
COL_VR = 2 * RET_QK
COL_GR = 2 * RET_QK + RET_V
COL_QA = 2 * RET_QK + 2 * RET_V
COL_KA = COL_QA + ATT_W
COL_VA = COL_KA + ATT_W
COL_GBR = COL_VA + ATT_W
COL_GBA = COL_GBR + D_MODEL

NEG_BIG = -1e30
PAGES_PER_BLOCK = MOBA_BLOCK // PAGE_SIZE
VMEM_LIMIT = 56 * 1024 * 1024


def _params(semantics):
    return pltpu.CompilerParams(dimension_semantics=semantics, vmem_limit_bytes=VMEM_LIMIT)


def _sigmoid(x):
    return 1.0 / (1.0 + jnp.exp(-x))


def _dot(a, b):
    return jnp.dot(a, b, preferred_element_type=F32)


def _dot_nt(a, b):
    return lax.dot_general(a, b, (((1,), (1,)), ((), ())), preferred_element_type=F32)


def _dot_tn(a, b):
    return lax.dot_general(a, b, (((0,), (0,)), ((), ())), preferred_element_type=F32)


def _layer_norm_rows(x, g, b):
    mu = jnp.mean(x, axis=-1, keepdims=True)
    xc = x - mu
    var = jnp.mean(xc * xc, axis=-1, keepdims=True)
    return xc * lax.rsqrt(var + LN_EPS) * g + b


IN_TN = 512
_J_ROT_END = (COL_VR) // IN_TN
_J_KR = COL_KR // IN_TN
_J_KA = COL_KA // IN_TN
_J_VA = COL_VA // IN_TN
_J_GBR = COL_GBR // IN_TN


def _in_proj_kernel(x_ref, w_ref, cos_ref, sin_ref, z_ref, k_ref, v_ref, xb_ref):
    j = pl.program_id(1)

    @pl.when(j == 0)
    def _():
        xb_ref[...] = x_ref[...].astype(BF16)

    acc = _dot(xb_ref[...], w_ref[...])

    @pl.when(j < _J_ROT_END)
    def _():
        cos = cos_ref[...]
        sin = sin_ref[...]
        scale = jnp.where(j >= _J_KR, DK_RET ** -0.5, 1.0).astype(F32)
        half = DK_RET // 2
        for hh in range(IN_TN // DK_RET):
            c0 = hh * DK_RET
            x1 = acc[:, c0:c0 + half]
            x2 = acc[:, c0 + half:c0 + DK_RET]
            z_ref[:, c0:c0 + half] = ((x1 * cos - x2 * sin) * scale).astype(z_ref.dtype)
            z_ref[:, c0 + half:c0 + DK_RET] = ((x1 * sin + x2 * cos) * scale).astype(z_ref.dtype)

    @pl.when(j >= _J_ROT_END)
    def _():
        z_ref[...] = acc.astype(z_ref.dtype)

    @pl.when((j >= _J_KA) & (j < _J_VA))
    def _():
        k_ref[...] = acc

    @pl.when((j >= _J_VA) & (j < _J_GBR))
    def _():
        v_ref[...] = acc


def _in_proj(x, w_bf, cos_t, sin_t, tm, z_dtype):
    m = x.shape[0]
    n_tab = cos_t.shape[0] // tm
    n_att = ATT_W // IN_TN
    return pl.pallas_call(
        _in_proj_kernel,
        grid=(m // tm, W_IN_COLS // IN_TN),
        in_specs=[
            pl.BlockSpec((tm, D_MODEL), lambda i, j: (i, 0)),
            pl.BlockSpec((D_MODEL, IN_TN), lambda i, j: (0, j)),
            pl.BlockSpec((tm, DK_RET // 2), lambda i, j: (i % n_tab, 0)),
            pl.BlockSpec((tm, DK_RET // 2), lambda i, j: (i % n_tab, 0)),
        ],
        out_specs=[
            pl.BlockSpec((tm, IN_TN), lambda i, j: (i, j)),
            pl.BlockSpec((tm, IN_TN), lambda i, j: (i, jnp.clip(j - _J_KA, 0, n_att - 1))),
            pl.BlockSpec((tm, IN_TN), lambda i, j: (i, jnp.clip(j - _J_VA, 0, n_att - 1))),
        ],
        out_shape=[
            jax.ShapeDtypeStruct((m, W_IN_COLS), z_dtype),
            jax.ShapeDtypeStruct((m, ATT_W), F32),
            jax.ShapeDtypeStruct((m, ATT_W), F32),
        ],
        scratch_shapes=[pltpu.VMEM((tm, D_MODEL), BF16)],
        compiler_params=_params(("arbitrary", "arbitrary")),
        name="in_proj",
    )(x, w_bf, cos_t, sin_t)


RET_HG = 4


def _retention_kernel(*refs, c_data, has_init):
    if has_init:
        (q_ref, k_ref, v_ref, g_ref, s0_ref, dec_ref, qd_ref, kd_ref, gc_ref, gain_ref,
         on_ref, s_ref, pad_ref) = refs
    else:
        (q_ref, k_ref, v_ref, g_ref, dec_ref, qd_ref, kd_ref, gc_ref, gain_ref,
         on_ref, s_ref, pad_ref) = refs
        s0_ref = None
    c = pl.program_id(2)
    padded = c_data != RET_CHUNK

    @pl.when(c == 0)
    def _():
        if has_init:
            s_ref[...] = s0_ref[...]
        else:
            s_ref[...] = jnp.zeros(s_ref.shape, F32)

    def load(ref, slot):
        if not padded:
            return ref[...]
        pad_ref[slot] = jnp.zeros(pad_ref.shape[1:], F32)
        pad_ref[slot, 0:c_data, :] = ref[...].astype(F32)
        return pad_ref[slot]

    q_all = load(q_ref, 0)
    k_all = load(k_ref, 1)
    v_all = load(v_ref, 2)
    g_all = g_ref[...]

    for hh in range(RET_HG):
        lo, hi = hh * DK_RET, (hh + 1) * DK_RET
        q = q_all[:, lo:hi].astype(BF16)
        k = k_all[:, lo:hi]
        v = v_all[:, lo:hi].astype(BF16)
        s_old = s_ref[0, hh]
        scores = _dot_nt(q, k.astype(BF16)) * dec_ref[hh]
        inner = _dot(scores.astype(BF16), v)
        cross = _dot(q, s_old.astype(BF16)) * qd_ref[hh]
        o = inner + cross
        kd = (k.astype(F32) * kd_ref[hh]).astype(BF16)
        s_ref[0, hh] = gc_ref[hh] * s_old + _dot_tn(kd, v)
        if padded:
            o = o[0:c_data]
        mu = jnp.mean(o, axis=-1, keepdims=True)
        oc = o - mu
        var = jnp.mean(oc * oc, axis=-1, keepdims=True)
        o_n = oc * lax.rsqrt(var + GN_EPS) * gain_ref[hh]
        g = g_all[:, lo:hi].astype(F32)
        on_ref[:, lo:hi] = (o_n * (g * _sigmoid(g))).astype(on_ref.dtype)


def _retention(z, s0, tabs, gain, n_batch, n_chunks, c_data, out_dtype):
    dec, qd, kd, gc = tabs
    m = z.shape[0]
    wblk = DK_RET * RET_HG
    has_init = s0 is not None

    def zspec(col0):
        base = col0 // wblk
        return pl.BlockSpec((c_data, wblk), lambda b, hg, c: (b * n_chunks + c, base + hg))

    tab3 = lambda shape: pl.BlockSpec((RET_HG,) + shape, lambda b, hg, c: (hg, 0, 0))
    sspec = pl.BlockSpec((1, RET_HG, DK_RET, DV_RET), lambda b, hg, c: (b, hg, 0, 0))
    in_specs = [zspec(COL_QR), zspec(COL_KR), zspec(COL_VR), zspec(COL_GR)]
    args = [z, z, z, z]
    if has_init:
        in_specs.append(sspec)
        args.append(s0)
    in_specs += [tab3((RET_CHUNK, RET_CHUNK)), tab3((RET_CHUNK, 1)), tab3((RET_CHUNK, 1)),
                 tab3((1, 1)), tab3((1, DV_RET))]
    args += [dec, qd, kd, gc, gain]
    return pl.pallas_call(
        functools.partial(_retention_kernel, c_data=c_data, has_init=has_init),
        grid=(n_batch, H_RET // RET_HG, n_chunks),
        in_specs=in_specs,
        out_specs=[
            pl.BlockSpec((c_data, wblk), lambda b, hg, c: (b * n_chunks + c, hg)),
            sspec,
        ],
        out_shape=[
            jax.ShapeDtypeStruct((m, RET_V), out_dtype),
            jax.ShapeDtypeStruct((n_batch, H_RET, DK_RET, DV_RET), F32),
        ],
        scratch_shapes=[pltpu.VMEM((3, RET_CHUNK, wblk), F32)],
        compiler_params=_params(("arbitrary", "arbitrary", "arbitrary")),
        name="retention",
    )(*args)


def _retention_tables(c_data):
    log_gamma = jnp.log1p(-jnp.exp2(-5.0 - jnp.arange(H_RET, dtype=F32)))
    idx = jnp.arange(RET_CHUNK, dtype=F32)
    live = idx < c_data
    diff = idx[:, None] - idx[None, :]
    causal = (diff >= 0) & live[:, None] & live[None, :]
    dec = jnp.where(causal, jnp.exp(log_gamma[:, None, None] * jnp.where(causal, diff, 0.0)), 0.0)
    qd = jnp.exp(log_gamma[:, None] * (idx[None, :] + 1.0))
    kd = jnp.where(live[None, :], jnp.exp(log_gamma[:, None] * (c_data - 1.0 - idx[None, :])), 0.0)
    gc = jnp.exp(log_gamma * c_data)
    return dec, qd[:, :, None], kd[:, :, None], gc[:, None, None]


def _moba_prompt_kernel(q_ref, k_ref, v_ref, o_ref, kb_ref, vb_ref, km_ref, m_ref, l_ref, acc_ref):
    i = pl.program_id(2)
    nb = k_ref.shape[0] // MOBA_BLOCK
    scale = HD_ATT ** -0.5

    @pl.when(i == 0)
    def _():
        kb_ref[...] = k_ref[...].astype(BF16)
        vb_ref[...] = v_ref[...].astype(BF16)
        km_ref[...] = jnp.zeros(km_ref.shape, F32)
        for n in range(nb):
            blk = k_ref[n * MOBA_BLOCK:(n + 1) * MOBA_BLOCK, :]
            km_ref[n:n + 1, :] = jnp.sum(blk, axis=0, keepdims=True) * (1.0 / MOBA_BLOCK)

    q = q_ref[...].astype(BF16)
    tq = q.shape[0]

    sm = _dot_nt(q, km_ref[...].astype(BF16))
    lane = lax.broadcasted_iota(jnp.int32, sm.shape, 1)
    s_el = jnp.where(lane < i, sm, -jnp.inf)
    cnt = jnp.zeros(sm.shape, jnp.int32)
    for n in range(nb):
        col = s_el[:, n:n + 1]
        ahead = (col > s_el) | ((col == s_el) & (lane > n))
        cnt = cnt + ahead.astype(jnp.int32)
    bias = jnp.where((lane < i) & (cnt < MOBA_TOPK), 0.0, NEG_BIG).astype(F32)

    k_own = kb_ref[pl.ds(pl.multiple_of(i * MOBA_BLOCK, MOBA_BLOCK), MOBA_BLOCK), :]
    v_own = vb_ref[pl.ds(pl.multiple_of(i * MOBA_BLOCK, MOBA_BLOCK), MOBA_BLOCK), :]
    s = _dot_nt(q, k_own) * scale
    row = lax.broadcasted_iota(jnp.int32, s.shape, 0)
    colv = lax.broadcasted_iota(jnp.int32, s.shape, 1)
    s = jnp.where(colv <= row, s, NEG_BIG)
    m0 = jnp.max(s, axis=-1, keepdims=True)
    p = jnp.exp(s - m0)
    m_ref[...] = m0
    l_ref[...] = jnp.sum(p, axis=-1, keepdims=True)
    acc_ref[...] = _dot(p.astype(BF16), v_own)

    for n in range(nb - 1):
        @pl.when(n < i)
        def _():
            kn = kb_ref[n * MOBA_BLOCK:(n + 1) * MOBA_BLOCK, :]
            vn = vb_ref[n * MOBA_BLOCK:(n + 1) * MOBA_BLOCK, :]
            sn = _dot_nt(q, kn) * scale + bias[:, n:n + 1]
            m_old = m_ref[...]
            m_new = jnp.maximum(m_old, jnp.max(sn, axis=-1, keepdims=True))
            a = jnp.exp(m_old - m_new)
            pn = jnp.exp(sn - m_new)
            m_ref[...] = m_new
            l_ref[...] = a * l_ref[...] + jnp.sum(pn, axis=-1, keepdims=True)
            acc_ref[...] = a * acc_ref[...] + _dot(pn.astype(BF16), vn)

    o_ref[...] = (acc_ref[...] / l_ref[...]).astype(o_ref.dtype)


def _moba_prompt(z, k_new, v_new, n_batch, seq):
    m = z.shape[0]
    nq = seq // MOBA_BLOCK
    qcol = COL_QA // HD_ATT
    return pl.pallas_call(
        _moba_prompt_kernel,
        grid=(n_batch, H_ATT, nq),
        in_specs=[
            pl.BlockSpec((MOBA_BLOCK, HD_ATT), lambda b, h, i: (b * nq + i, qcol + h)),
            pl.BlockSpec((seq, HD_ATT), lambda b, h, i: (b, h)),
            pl.BlockSpec((seq, HD_ATT), lambda b, h, i: (b, h)),
        ],
        out_specs=pl.BlockSpec((MOBA_BLOCK, HD_ATT), lambda b, h, i: (b * nq + i, h)),
        out_shape=jax.ShapeDtypeStruct((m, ATT_W), z.dtype),
        scratch_shapes=[
            pltpu.VMEM((seq, HD_ATT), BF16),
            pltpu.VMEM((seq, HD_ATT), BF16),
            pltpu.VMEM((128, HD_ATT), F32),
            pltpu.VMEM((MOBA_BLOCK, 1), F32),
            pltpu.VMEM((MOBA_BLOCK, 1), F32),
            pltpu.VMEM((MOBA_BLOCK, HD_ATT), F32),
        ],
        compiler_params=_params(("arbitrary", "arbitrary", "arbitrary")),
        name="moba_prompt",
    )(z, k_new, v_new)


SMP_G = 2


def _moba_sample_kernel(pt_ref, q_ref, kn_ref, vn_ref, *rest, n_tok, n_past):
    n_pg = SMP_G * PAGES_PER_BLOCK
    ck = rest[:n_pg]
    cv = rest[n_pg:2 * n_pg]
    o_ref, m_all, l_all, s_all, o_sc, kpad, vpad = rest[2 * n_pg:]
    s_id = pl.program_id(1)
    scale = HD_ATT ** -0.5
    rows = H_ATT * n_tok
    pg_rows = PAGE_SIZE * H_ATT
    tok_shift = n_tok.bit_length() - 1
    assert n_tok == 1 << tok_shift and H_ATT & (H_ATT - 1) == 0 and rows <= 128

    @pl.when(s_id == 0)
    def _():
        m_all[...] = jnp.zeros(m_all.shape, F32)
        l_all[...] = jnp.zeros(l_all.shape, F32)
        s_all[...] = jnp.zeros(s_all.shape, F32)

    def by_head(ref):
        x = ref[...]
        return jnp.concatenate([x[:, h * HD_ATT:(h + 1) * HD_ATT] for h in range(H_ATT)], axis=0)

    qb = by_head(q_ref).astype(BF16)
    qr = qb.astype(F32)
    row_head = lax.broadcasted_iota(jnp.int32, (rows, pg_rows), 0) >> tok_shift
    col_head = lax.broadcasted_iota(jnp.int32, (rows, pg_rows), 1) & (H_ATT - 1)
    same_head = row_head == col_head
    lane = lax.broadcasted_iota(jnp.int32, (rows, 128), 1)

    for g in range(SMP_G):
        n = s_id * SMP_G + g
        k0 = ck[2 * g][...]
        k1 = ck[2 * g + 1][...]
        ksum = (jnp.sum(k0.reshape(PAGE_SIZE, H_ATT, HD_ATT), axis=0)
                + jnp.sum(k1.reshape(PAGE_SIZE, H_ATT, HD_ATT), axis=0))
        kmean = (ksum * (1.0 / MOBA_BLOCK)).astype(BF16).astype(F32)
        kexp = jnp.concatenate(
            [jnp.broadcast_to(kmean[h:h + 1, :], (n_tok, HD_ATT)) for h in range(H_ATT)], axis=0)
        sc = jnp.sum(qr * kexp, axis=-1, keepdims=True)
        s0 = jnp.where(same_head, _dot_nt(qb, k0.astype(BF16)) * scale, NEG_BIG)
        s1 = jnp.where(same_head, _dot_nt(qb, k1.astype(BF16)) * scale, NEG_BIG)
        mh = jnp.maximum(jnp.max(s0, axis=-1, keepdims=True), jnp.max(s1, axis=-1, keepdims=True))
        p0 = jnp.exp(s0 - mh)
        p1 = jnp.exp(s1 - mh)
        lh = jnp.sum(p0, axis=-1, keepdims=True) + jnp.sum(p1, axis=-1, keepdims=True)
        o_sc[n] = (_dot(p0.astype(BF16), cv[2 * g][...].astype(BF16))
                   + _dot(p1.astype(BF16), cv[2 * g + 1][...].astype(BF16)))
        here = lane == n
        m_all[...] = jnp.where(here, mh, m_all[...])
        l_all[...] = jnp.where(here, lh, l_all[...])
        s_all[...] = jnp.where(here, sc, s_all[...])

    @pl.when(s_id == pl.num_programs(1) - 1)
    def _():
        past = lane < n_past
        sc_all = jnp.where(past, s_all[...], -jnp.inf)
        cnt = jnp.zeros(sc_all.shape, jnp.int32)
        for nn in range(n_past):
            col = sc_all[:, nn:nn + 1]
            ahead = (col > sc_all) | ((col == sc_all) & (lane > nn))
            cnt = cnt + ahead.astype(jnp.int32)
        sel = past & (cnt < MOBA_TOPK)
        m_sel = jnp.where(sel, m_all[...], NEG_BIG)

        kpad[...] = jnp.zeros(kpad.shape, F32)
        vpad[...] = jnp.zeros(vpad.shape, F32)
        kpad[0:rows, :] = by_head(kn_ref)
        vpad[0:rows, :] = by_head(vn_ref)
        r_id = lax.broadcasted_iota(jnp.int32, (rows, 128), 0)
        own_ok = ((lane < rows) & ((lane >> tok_shift) == (r_id >> tok_shift))
                  & ((lane & (n_tok - 1)) <= (r_id & (n_tok - 1))))
        s_own = jnp.where(own_ok, _dot_nt(qb, kpad[...].astype(BF16)) * scale, NEG_BIG)
        big = jnp.maximum(jnp.max(m_sel, axis=-1, keepdims=True), jnp.max(s_own, axis=-1, keepdims=True))
        w = jnp.where(sel, jnp.exp(m_sel - big), 0.0)
        p_own = jnp.exp(s_own - big)
        den = jnp.sum(w * l_all[...], axis=-1, keepdims=True) + jnp.sum(p_own, axis=-1, keepdims=True)
        num = _dot(p_own.astype(BF16), vpad[...].astype(BF16))
        for nn in range(n_past):
            num = num + w[:, nn:nn + 1] * o_sc[nn]
        out = num / den
        for h in range(H_ATT):
            o_ref[:, h * HD_ATT:(h + 1) * HD_ATT] = out[h * n_tok:(h + 1) * n_tok, :].astype(o_ref.dtype)


def _moba_sample(z, k_new, v_new, cache_k, cache_v, page_table, n_batch, n_tok):
    n_pages = page_table.shape[1]
    n_past = n_pages // PAGES_PER_BLOCK
    n_pg = SMP_G * PAGES_PER_BLOCK
    n_steps = n_past // SMP_G
    pt = page_table.reshape(-1)
    ck = cache_k.reshape(cache_k.shape[0], PAGE_SIZE * H_ATT, HD_ATT)
    cv = cache_v.reshape(cache_v.shape[0], PAGE_SIZE * H_ATT, HD_ATT)

    def page_spec(jj):
        return pl.BlockSpec((None, PAGE_SIZE * H_ATT, HD_ATT),
                            lambda b, s, pt_ref: (pt_ref[b * n_pages + s * n_pg + jj], 0, 0))

    row = lambda col: pl.BlockSpec((n_tok, ATT_W), lambda b, s, pt_ref: (b, col))
    rows = H_ATT * n_tok
    grid_spec = pltpu.PrefetchScalarGridSpec(
        num_scalar_prefetch=1,
        grid=(n_batch, n_steps),
        in_specs=[row(COL_QA // ATT_W), row(0), row(0)]
        + [page_spec(jj) for jj in range(n_pg)] + [page_spec(jj) for jj in range(n_pg)],
        out_specs=row(0),
        scratch_shapes=[
            pltpu.VMEM((rows, 128), F32),
            pltpu.VMEM((rows, 128), F32),
            pltpu.VMEM((rows, 128), F32),
            pltpu.VMEM((n_past, rows, HD_ATT), F32),
            pltpu.VMEM((128, HD_ATT), F32),
            pltpu.VMEM((128, HD_ATT), F32),
        ],
    )
    return pl.pallas_call(
        functools.partial(_moba_sample_kernel, n_tok=n_tok, n_past=n_past),
        grid_spec=grid_spec,
        out_shape=jax.ShapeDtypeStruct((z.shape[0], ATT_W), z.dtype),
        compiler_params=_params(("arbitrary", "arbitrary")),
        name="moba_sample",
    )(pt, z, k_new, v_new, *([ck] * n_pg), *([cv] * n_pg))


MIX_TN = 512


def _gate_mix_kernel(on_ref, oa_ref, wr_ref, wa_ref, gbr_ref, gba_ref, o_ref):
    ret = _dot(on_ref[...].astype(BF16), wr_ref[...])
    att = _dot(oa_ref[...].astype(BF16), wa_ref[...])
    mix = _sigmoid(gbr_ref[...].astype(F32)) * ret + _sigmoid(gba_ref[...].astype(F32)) * att
    o_ref[...] = mix.astype(o_ref.dtype)


def _gate_mix(on, oa, z, w_ret, w_att, tm):
    m = on.shape[0]
    jr = COL_GBR // MIX_TN
    ja = COL_GBA // MIX_TN
    return pl.pallas_call(
        _gate_mix_kernel,
        grid=(m // tm, D_MODEL // MIX_TN),
        in_specs=[
            pl.BlockSpec((tm, RET_V), lambda i, j: (i, 0)),
            pl.BlockSpec((tm, ATT_W), lambda i, j: (i, 0)),
            pl.BlockSpec((RET_V, MIX_TN), lambda i, j: (0, j)),
            pl.BlockSpec((ATT_W, MIX_TN), lambda i, j: (0, j)),
            pl.BlockSpec((tm, MIX_TN), lambda i, j: (i, jr + j)),
            pl.BlockSpec((tm, MIX_TN), lambda i, j: (i, ja + j)),
        ],
        out_specs=pl.BlockSpec((tm, MIX_TN), lambda i, j: (i, j)),
        out_shape=jax.ShapeDtypeStruct((m, D_MODEL), BF16),
        compiler_params=_params(("arbitrary", "arbitrary")),
        name="gate_mix",
    )(on, oa, w_ret, w_att, z, z)


def _out_ln_kernel(mix_ref, w_ref, x_ref, g_ref, b_ref, h_ref, hb_ref):
    mix = _dot(mix_ref[...], w_ref[...])
    h = _layer_norm_rows(ALPHA * x_ref[...] + mix, g_ref[...], b_ref[...])
    h_ref[...] = h
    hb_ref[...] = h.astype(BF16)


def _out_ln(mix_in, w_out, x, g, b, tm):
    m = x.shape[0]
    rowspec = pl.BlockSpec((tm, D_MODEL), lambda i: (i, 0))
    vec = pl.BlockSpec((1, D_MODEL), lambda i: (0, 0))
    return pl.pallas_call(
        _out_ln_kernel,
        grid=(m // tm,),
        in_specs=[rowspec, pl.BlockSpec((D_MODEL, D_MODEL), lambda i: (0, 0)), rowspec, vec, vec],
        out_specs=[rowspec, rowspec],
        out_shape=[jax.ShapeDtypeStruct((m, D_MODEL), F32), jax.ShapeDtypeStruct((m, D_MODEL), BF16)],
        compiler_params=_params(("arbitrary",)),
        name="out_ln",
    )(mix_in, w_out, x, g, b)


MLP_TF = 512


def _mlp_ln_kernel(h_ref, hb_ref, wu_ref, wd_ref, g_ref, b_ref, y_ref, acc_ref):
    f = pl.program_id(1)
    u = _dot(hb_ref[...], wu_ref[...])
    a = jnp.maximum(u, 0.0)
    part = _dot((a * a).astype(BF16), wd_ref[...])

    @pl.when(f == 0)
    def _():
        acc_ref[...] = part

    @pl.when(f > 0)
    def _():
        acc_ref[...] += part

    @pl.when(f == pl.num_programs(1) - 1)
    def _():
        y_ref[...] = _layer_norm_rows(ALPHA * h_ref[...] + acc_ref[...], g_ref[...], b_ref[...])


def _mlp_ln(h, hb, w_up, w_down, g, b, tm):
    m = h.shape[0]
    rowspec = pl.BlockSpec((tm, D_MODEL), lambda i, f: (i, 0))
    vec = pl.BlockSpec((1, D_MODEL), lambda i, f: (0, 0))
    return pl.pallas_call(
        _mlp_ln_kernel,
        grid=(m // tm, D_FF // MLP_TF),
        in_specs=[
            rowspec, rowspec,
            pl.BlockSpec((D_MODEL, MLP_TF), lambda i, f: (0, f)),
            pl.BlockSpec((MLP_TF, D_MODEL), lambda i, f: (f, 0)),
            vec, vec,
        ],
        out_specs=rowspec,
        out_shape=jax.ShapeDtypeStruct((m, D_MODEL), F32),
        scratch_shapes=[pltpu.VMEM((tm, D_MODEL), F32)],
        compiler_params=_params(("arbitrary", "arbitrary")),
        name="mlp_ln",
    )(h, hb, w_up, w_down, g, b)


def _rotary_tables(pos):
    half = DK_RET // 2
    inv = ROPE_BASE ** (-jnp.arange(half, dtype=F32) / half)
    ang = pos.astype(F32)[:, None] * inv[None, :]
    return jnp.cos(ang), jnp.sin(ang)


def _merge(x, z, on, oa, w, tm):
    mix_in = _gate_mix(on, oa, z, w["ret"], w["att"], tm)
    h, hb = _out_ln(mix_in, w["out"], x, w["ln1_g"], w["ln1_b"], tm)
    return _mlp_ln(h, hb, w["up"], w["down"], w["ln2_g"], w["ln2_b"], tm)


def kernel(x_prompt, x_sample, cache_k, cache_v, state_ret, page_table, w_in, ret_gn_gain, w_ret_br,
           w_att_br, w_out, ln1_g, ln1_b, w_up, w_down, ln2_g, ln2_b):
    n_b, seq, _ = x_prompt.shape
    n_db, n_tok, _ = x_sample.shape
    assert w_in.shape[0] == DEPTH == 1
    l = 0
    w = {
        "in": w_in[l].astype(BF16), "ret": w_ret_br[l].astype(BF16), "att": w_att_br[l].astype(BF16),
        "out": w_out[l].astype(BF16), "up": w_up[l].astype(BF16), "down": w_down[l].astype(BF16),
        "ln1_g": ln1_g[l][None, :], "ln1_b": ln1_b[l][None, :],
        "ln2_g": ln2_g[l][None, :], "ln2_b": ln2_b[l][None, :],
    }
    gain = ret_gn_gain[l][:, None, :]

    xp = x_prompt.reshape(n_b * seq, D_MODEL)
    cos_p, sin_p = _rotary_tables(jnp.arange(seq, dtype=jnp.int32))
    tm_p = 1024
    z_p, k_p, v_p = _in_proj(xp, w["in"], cos_p, sin_p, tm_p, BF16)
    on_p, s_p = _retention(z_p, None, _retention_tables(RET_CHUNK), gain, n_b, seq // RET_CHUNK,
                           RET_CHUNK, BF16)
    oa_p = _moba_prompt(z_p, k_p, v_p, n_b, seq)
    y_p = _merge(xp, z_p, on_p, oa_p, w, 512)

    xs = x_sample.reshape(n_db * n_tok, D_MODEL)
    pos_s = PAST_LEN + jnp.arange(n_tok, dtype=jnp.int32)
    cos_s, sin_s = _rotary_tables(jnp.tile(pos_s, n_db))
    tm_s = n_db * n_tok
    z_s, k_s, v_s = _in_proj(xs, w["in"], cos_s, sin_s, tm_s, F32)
    on_s, s_s = _retention(z_s, state_ret[l], _retention_tables(n_tok), gain, n_db, 1, n_tok, F32)
    oa_s = _moba_sample(z_s, k_s, v_s, cache_k[l], cache_v[l], page_table, n_db, n_tok)
    y_s = _merge(xs, z_s, on_s, oa_s, w, tm_s)

    return (
        y_p.reshape(n_b, seq, D_MODEL),
        y_s.reshape(n_db, n_tok, D_MODEL),
        k_p.reshape(1, n_b, seq, H_ATT, HD_ATT),
        v_p.reshape(1, n_b, seq, H_ATT, HD_ATT),
        s_p[None],
        k_s.reshape(1, n_db, n_tok, H_ATT, HD_ATT),
        v_s.reshape(1, n_db, n_tok, H_ATT, HD_ATT),
        s_s[None],
    )
```

```python
import functools
import math

import jax
import jax.numpy as jnp
from jax import lax
from jax.experimental import pallas as pl
from jax.experimental.pallas import tpu as pltpu

F32 = jnp.float32
BF16 = jnp.bfloat16

D_MODEL = 2048
DEPTH = 1
PAST_LEN = 8192
PAGE_SIZE = 128
H_RET = 8
DK_RET = 256
DV_RET = 256
RET_CHUNK = 128
H_ATT = 8
HD_ATT = 128
MOBA_BLOCK = 256
MOBA_TOPK = 3
D_FF = 4 * D_MODEL
RET_QK = H_RET * DK_RET
RET_V = H_RET * DV_RET
ATT_W = H_ATT * HD_ATT
W_IN_COLS = 2 * RET_QK + 2 * RET_V + 3 * ATT_W + 2 * D_MODEL
ROPE_BASE = 10000.0
LN_EPS = 1e-5
GN_EPS = 1e-5
ALPHA = (2 * DEPTH) ** 0.25

COL_QR = 0
COL_KR = RET_QK
COL_VR = 2 * RET_QK
COL_GR = 2 * RET_QK + RET_V
COL_QA = 2 * RET_QK + 2 * RET_V
COL_KA = COL_QA + ATT_W
COL_VA = COL_KA + ATT_W
COL_GBR = COL_VA + ATT_W
COL_GBA = COL_GBR + D_MODEL

NEG_BIG = -1e30
SOFTMAX_C = HD_ATT ** -0.5 * math.log2(math.e)
PAGES_PER_BLOCK = MOBA_BLOCK // PAGE_SIZE
VMEM_LIMIT = 56 * 1024 * 1024


def _params(semantics):
    return pltpu.CompilerParams(dimension_semantics=semantics, vmem_limit_bytes=VMEM_LIMIT)


def _sigmoid(x):
    return 1.0 / (1.0 + jnp.exp(-x))


def _dot(a, b):
    return jnp.dot(a, b, preferred_element_type=F32)


def _dot_nt(a, b):
    return lax.dot_general(a, b, (((1,), (1,)), ((), ())), preferred_element_type=F32)


def _dot_tn(a, b):
    return lax.dot_general(a, b, (((0,), (0,)), ((), ())), preferred_element_type=F32)


def _layer_norm_rows(x, g, b):
    mu = jnp.mean(x, axis=-1, keepdims=True)
    xc = x - mu
    var = jnp.mean(xc * xc, axis=-1, keepdims=True)
    return xc * lax.rsqrt(var + LN_EPS) * g + b


IN_TN = 512
_J_ROT_END = (COL_VR) // IN_TN
_J_KR = COL_KR // IN_TN
_J_KA = COL_KA // IN_TN
_J_VA = COL_VA // IN_TN
_J_GBR = COL_GBR // IN_TN


def _in_proj_kernel(x_ref, w_ref, cos_ref, sin_ref, z_ref, k_ref, v_ref, xb_ref):
    j = pl.program_id(1)

    @pl.when(j == 0)
    def _():
        xb_ref[...] = x_ref[...].astype(BF16)

    acc = _dot(xb_ref[...], w_ref[...])

    @pl.when(j < _J_ROT_END)
    def _():
        cos = cos_ref[...]
        sin = sin_ref[...]
        scale = jnp.where(j >= _J_KR, DK_RET ** -0.5, 1.0).astype(F32)
        half = DK_RET // 2
        for hh in range(IN_TN // DK_RET):
            c0 = hh * DK_RET
            x1 = acc[:, c0:c0 + half]
            x2 = acc[:, c0 + half:c0 + DK_RET]
            z_ref[:, c0:c0 + half] = ((x1 * cos - x2 * sin) * scale).astype(z_ref.dtype)
            z_ref[:, c0 + half:c0 + DK_RET] = ((x1 * sin + x2 * cos) * scale).astype(z_ref.dtype)

    @pl.when(j >= _J_ROT_END)
    def _():
        z_ref[...] = acc.astype(z_ref.dtype)

    @pl.when((j >= _J_KA) & (j < _J_VA))
    def _():
        k_ref[...] = acc

    @pl.when((j >= _J_VA) & (j < _J_GBR))
    def _():
        v_ref[...] = acc


def _in_proj(x, w_bf, cos_t, sin_t, tm, z_dtype):
    m = x.shape[0]
    n_tab = cos_t.shape[0] // tm
    n_att = ATT_W // IN_TN
    return pl.pallas_call(
        _in_proj_kernel,
        grid=(m // tm, W_IN_COLS // IN_TN),
        in_specs=[
            pl.BlockSpec((tm, D_MODEL), lambda i, j: (i, 0)),
            pl.BlockSpec((D_MODEL, IN_TN), lambda i, j: (0, j)),
            pl.BlockSpec((tm, DK_RET // 2), lambda i, j: (i % n_tab, 0)),
            pl.BlockSpec((tm, DK_RET // 2), lambda i, j: (i % n_tab, 0)),
        ],
        out_specs=[
            pl.BlockSpec((tm, IN_TN), lambda i, j: (i, j)),
            pl.BlockSpec((tm, IN_TN), lambda i, j: (i, jnp.clip(j - _J_KA, 0, n_att - 1))),
            pl.BlockSpec((tm, IN_TN), lambda i, j: (i, jnp.clip(j - _J_VA, 0, n_att - 1))),
        ],
        out_shape=[
            jax.ShapeDtypeStruct((m, W_IN_COLS), z_dtype),
            jax.ShapeDtypeStruct((m, ATT_W), F32),
            jax.ShapeDtypeStruct((m, ATT_W), F32),
        ],
        scratch_shapes=[pltpu.VMEM((tm, D_MODEL), BF16)],
        compiler_params=_params(("arbitrary", "arbitrary")),
        name="in_proj",
    )(x, w_bf, cos_t, sin_t)


def _retention_kernel(*refs, c_data, has_init, n_heads):
    if has_init:
        (q_ref, k_ref, v_ref, g_ref, s0_ref, dec_ref, qd_ref, kd_ref, gc_ref, gain_ref,
         on_ref, s_ref, pad_ref) = refs
    else:
        (q_ref, k_ref, v_ref, g_ref, dec_ref, qd_ref, kd_ref, gc_ref, gain_ref,
         on_ref, s_ref, pad_ref) = refs
        s0_ref = None
    c = pl.program_id(2)
    padded = c_data != RET_CHUNK

    @pl.when(c == 0)
    def _():
        if has_init:
            s_ref[...] = s0_ref[...]
        else:
            s_ref[...] = jnp.zeros(s_ref.shape, F32)

    def load(ref, slot):
        if not padded:
            return ref[...]
        pad_ref[slot] = jnp.zeros(pad_ref.shape[1:], F32)
        pad_ref[slot, 0:c_data, :] = ref[...].astype(F32)
        return pad_ref[slot]

    q_all = load(q_ref, 0)
    k_all = load(k_ref, 1)
    v_all = load(v_ref, 2)
    g_all = g_ref[...]

    for hh in range(n_heads):
        lo, hi = hh * DK_RET, (hh + 1) * DK_RET
        q = q_all[:, lo:hi].astype(BF16)
        k = k_all[:, lo:hi]
        v = v_all[:, lo:hi].astype(BF16)
        s_old = s_ref[0, hh]
        scores = _dot_nt(q, k.astype(BF16)) * dec_ref[hh]
        inner = _dot(scores.astype(BF16), v)
        cross = _dot(q, s_old.astype(BF16)) * qd_ref[hh]
        o = inner + cross
        kd = (k.astype(F32) * kd_ref[hh]).astype(BF16)
        s_ref[0, hh] = gc_ref[hh] * s_old + _dot_tn(kd, v)
        if padded:
            o = o[0:c_data]
        mu = jnp.mean(o, axis=-1, keepdims=True)
        oc = o - mu
        var = jnp.mean(oc * oc, axis=-1, keepdims=True)
        o_n = oc * lax.rsqrt(var + GN_EPS) * gain_ref[hh]
        g = g_all[:, lo:hi].astype(F32)
        on_ref[:, lo:hi] = (o_n * (g * _sigmoid(g))).astype(on_ref.dtype)


def _retention(z, s0, tabs, gain, n_batch, n_chunks, c_data, out_dtype, n_heads):
    dec, qd, kd, gc = tabs
    m = z.shape[0]
    wblk = DK_RET * n_heads
    has_init = s0 is not None

    def zspec(col0):
        base = col0 // wblk
        return pl.BlockSpec((c_data, wblk), lambda b, hg, c: (b * n_chunks + c, base + hg))

    tab3 = lambda shape: pl.BlockSpec((n_heads,) + shape, lambda b, hg, c: (hg, 0, 0))
    sspec = pl.BlockSpec((1, n_heads, DK_RET, DV_RET), lambda b, hg, c: (b, hg, 0, 0))
    in_specs = [zspec(COL_QR), zspec(COL_KR), zspec(COL_VR), zspec(COL_GR)]
    args = [z, z, z, z]
    if has_init:
        in_specs.append(sspec)
        args.append(s0)
    in_specs += [tab3((RET_CHUNK, RET_CHUNK)), tab3((RET_CHUNK, 1)), tab3((RET_CHUNK, 1)),
                 tab3((1, 1)), tab3((1, DV_RET))]
    args += [dec, qd, kd, gc, gain]
    return pl.pallas_call(
        functools.partial(_retention_kernel, c_data=c_data, has_init=has_init, n_heads=n_heads),
        grid=(n_batch, H_RET // n_heads, n_chunks),
        in_specs=in_specs,
        out_specs=[
            pl.BlockSpec((c_data, wblk), lambda b, hg, c: (b * n_chunks + c, hg)),
            sspec,
        ],
        out_shape=[
            jax.ShapeDtypeStruct((m, RET_V), out_dtype),
            jax.ShapeDtypeStruct((n_batch, H_RET, DK_RET, DV_RET), F32),
        ],
        scratch_shapes=[pltpu.VMEM((3, RET_CHUNK, wblk), F32)],
        compiler_params=_params(("arbitrary", "arbitrary", "arbitrary")),
        name="retention",
    )(*args)


def _retention_tables(c_data):
    log_gamma = jnp.log1p(-jnp.exp2(-5.0 - jnp.arange(H_RET, dtype=F32)))
    idx = jnp.arange(RET_CHUNK, dtype=F32)
    live = idx < c_data
    diff = idx[:, None] - idx[None, :]
    causal = (diff >= 0) & live[:, None] & live[None, :]
    dec = jnp.where(causal, jnp.exp(log_gamma[:, None, None] * jnp.where(causal, diff, 0.0)), 0.0)
    qd = jnp.exp(log_gamma[:, None] * (idx[None, :] + 1.0))
    kd = jnp.where(live[None, :], jnp.exp(log_gamma[:, None] * (c_data - 1.0 - idx[None, :])), 0.0)
    gc = jnp.exp(log_gamma * c_data)
    return dec, qd[:, :, None], kd[:, :, None], gc[:, None, None]


def _moba_prompt_kernel(q_ref, k_ref, v_ref, o_ref, kb_ref, vb_ref, km_ref):
    nb = k_ref.shape[0] // MOBA_BLOCK
    blk = MOBA_BLOCK

    kb_ref[...] = k_ref[...].astype(BF16)
    vb_ref[...] = v_ref[...].astype(BF16)
    km_ref[...] = jnp.zeros(km_ref.shape, F32)
    for n in range(nb):
        km_ref[n:n + 1, :] = jnp.sum(k_ref[n * blk:(n + 1) * blk, :], axis=0, keepdims=True) * (1.0 / blk)
    kmb = km_ref[...].astype(BF16)

    lane = lax.broadcasted_iota(jnp.int32, (blk, 128), 1)
    row = lax.broadcasted_iota(jnp.int32, (blk, blk), 0)
    col = lax.broadcasted_iota(jnp.int32, (blk, blk), 1)
    causal_bias = jnp.where(col <= row, 0.0, NEG_BIG).astype(F32)

    for i in range(nb):
        q = q_ref[i * blk:(i + 1) * blk, :].astype(BF16)
        pieces = []
        if i > 0:
            s_el = jnp.where(lane < i, _dot_nt(q, kmb), -jnp.inf)
            cnt = jnp.zeros(s_el.shape, jnp.int32)
            for n in range(i):
                cn = s_el[:, n:n + 1]
                ahead = (cn > s_el) | ((cn == s_el) & (lane > n))
                cnt = cnt + ahead.astype(jnp.int32)
            bias = jnp.where((lane < i) & (cnt < MOBA_TOPK), 0.0, NEG_BIG).astype(F32)
            pieces = [jnp.broadcast_to(bias[:, n:n + 1], (blk, blk)) for n in range(i)]
        nk = (i + 1) * blk
        s = _dot_nt(q, kb_ref[0:nk, :]) + jnp.concatenate(pieces + [causal_bias], axis=1)
        m = jnp.max(s, axis=-1, keepdims=True)
        p = jnp.exp2((s - m) * SOFTMAX_C)
        l = jnp.sum(p, axis=-1, keepdims=True)
        o = _dot(p.astype(BF16), vb_ref[0:nk, :]) / l
        o_ref[i * blk:(i + 1) * blk, :] = o.astype(o_ref.dtype)


def _moba_prompt(z, k_new, v_new, n_batch, seq):
    m = z.shape[0]
    qcol = COL_QA // HD_ATT
    head_rows = lambda col0: pl.BlockSpec((seq, HD_ATT), lambda b, h: (b, col0 + h))
    return pl.pallas_call(
        _moba_prompt_kernel,
        grid=(n_batch, H_ATT),
        in_specs=[head_rows(qcol), head_rows(0), head_rows(0)],
        out_specs=head_rows(0),
        out_shape=jax.ShapeDtypeStruct((m, ATT_W), z.dtype),
        scratch_shapes=[
            pltpu.VMEM((seq, HD_ATT), BF16),
            pltpu.VMEM((seq, HD_ATT), BF16),
            pltpu.VMEM((128, HD_ATT), F32),
        ],
        compiler_params=_params(("arbitrary", "arbitrary")),
        name="moba_prompt",
    )(z, k_new, v_new)


SMP_G = 4


def _moba_sample_kernel(pt_ref, q_ref, kn_ref, vn_ref, hb_ref, *rest, n_tok, n_past):
    n_pg = SMP_G * PAGES_PER_BLOCK
    ck = rest[:n_pg]
    cv = rest[n_pg:2 * n_pg]
    o_ref, m_all, l_all, s_all, o_sc, kpad, vpad = rest[2 * n_pg:]
    s_id = pl.program_id(1)
    rows = H_ATT * n_tok
    tok_shift = n_tok.bit_length() - 1
    assert n_tok == 1 << tok_shift and H_ATT & (H_ATT - 1) == 0 and rows <= 128

    @pl.when(s_id == 0)
    def _():
        m_all[...] = jnp.zeros(m_all.shape, F32)
        l_all[...] = jnp.zeros(l_all.shape, F32)
        s_all[...] = jnp.zeros(s_all.shape, F32)

    def by_head(ref):
        x = ref[...]
        return jnp.concatenate([x[:, h * HD_ATT:(h + 1) * HD_ATT] for h in range(H_ATT)], axis=0)

    qb = by_head(q_ref).astype(BF16)
    qr = qb.astype(F32)
    hb = hb_ref[...]
    lane = lax.broadcasted_iota(jnp.int32, (rows, 128), 1)

    for g in range(SMP_G):
        n = s_id * SMP_G + g
        k0 = ck[2 * g][...]
        k1 = ck[2 * g + 1][...]
        ksum = (jnp.sum(k0.reshape(PAGE_SIZE, H_ATT, HD_ATT), axis=0)
                + jnp.sum(k1.reshape(PAGE_SIZE, H_ATT, HD_ATT), axis=0))
        kmean = (ksum * (1.0 / MOBA_BLOCK)).astype(BF16).astype(F32)
        kexp = jnp.concatenate(
            [jnp.broadcast_to(kmean[h:h + 1, :], (n_tok, HD_ATT)) for h in range(H_ATT)], axis=0)
        sc = jnp.sum(qr * kexp, axis=-1, keepdims=True)
        s0 = _dot_nt(qb, k0.astype(BF16)) + hb
        s1 = _dot_nt(qb, k1.astype(BF16)) + hb
        mh = jnp.maximum(jnp.max(s0, axis=-1, keepdims=True), jnp.max(s1, axis=-1, keepdims=True))
        p0 = jnp.exp2((s0 - mh) * SOFTMAX_C)
        p1 = jnp.exp2((s1 - mh) * SOFTMAX_C)
        lh = jnp.sum(p0, axis=-1, keepdims=True) + jnp.sum(p1, axis=-1, keepdims=True)
        o_sc[n] = (_dot(p0.astype(BF16), cv[2 * g][...].astype(BF16))
                   + _dot(p1.astype(BF16), cv[2 * g + 1][...].astype(BF16)))
        here = lane == n
        m_all[...] = jnp.where(here, mh, m_all[...])
        l_all[...] = jnp.where(here, lh, l_all[...])
        s_all[...] = jnp.where(here, sc, s_all[...])

    @pl.when(s_id == pl.num_programs(1) - 1)
    def _():
        past = lane < n_past
        sc_all = jnp.where(past, s_all[...], -jnp.inf)
        cnt = jnp.zeros(sc_all.shape, jnp.int32)
        for nn in range(n_past):
            col = sc_all[:, nn:nn + 1]
            ahead = (col > sc_all) | ((col == sc_all) & (lane > nn))
            cnt = cnt + ahead.astype(jnp.int32)
        sel = past & (cnt < MOBA_TOPK)
        m_sel = jnp.where(sel, m_all[...], NEG_BIG)

        kpad[...] = jnp.zeros(kpad.shape, F32)
        vpad[...] = jnp.zeros(vpad.shape, F32)
        kpad[0:rows, :] = by_head(kn_ref)
        vpad[0:rows, :] = by_head(vn_ref)
        r_id = lax.broadcasted_iota(jnp.int32, (rows, 128), 0)
        own_ok = ((lane < rows) & ((lane >> tok_shift) == (r_id >> tok_shift))
                  & ((lane & (n_tok - 1)) <= (r_id & (n_tok - 1))))
        s_own = jnp.where(own_ok, _dot_nt(qb, kpad[...].astype(BF16)), NEG_BIG)
        big = jnp.maximum(jnp.max(m_sel, axis=-1, keepdims=True), jnp.max(s_own, axis=-1, keepdims=True))
        w = jnp.where(sel, jnp.exp2((m_sel - big) * SOFTMAX_C), 0.0)
        p_own = jnp.exp2((s_own - big) * SOFTMAX_C)
        den = jnp.sum(w * l_all[...], axis=-1, keepdims=True) + jnp.sum(p_own, axis=-1, keepdims=True)
        num = _dot(p_own.astype(BF16), vpad[...].astype(BF16))
        for nn in range(n_past):
            num = num + w[:, nn:nn + 1] * o_sc[nn]
        out = num / den
        for h in range(H_ATT):
            o_ref[:, h * HD_ATT:(h + 1) * HD_ATT] = out[h * n_tok:(h + 1) * n_tok, :].astype(o_ref.dtype)


def _moba_sample(z, k_new, v_new, cache_k, cache_v, page_table, n_batch, n_tok):
    n_pages = page_table.shape[1]
    n_past = n_pages // PAGES_PER_BLOCK
    n_pg = SMP_G * PAGES_PER_BLOCK
    n_steps = n_past // SMP_G
    pt = page_table.reshape(-1)
    ck = cache_k.reshape(cache_k.shape[0], PAGE_SIZE * H_ATT, HD_ATT)
    cv = cache_v.reshape(cache_v.shape[0], PAGE_SIZE * H_ATT, HD_ATT)

    def page_spec(jj):
        return pl.BlockSpec((None, PAGE_SIZE * H_ATT, HD_ATT),
                            lambda b, s, pt_ref: (pt_ref[b * n_pages + s * n_pg + jj], 0, 0))

    row = lambda col: pl.BlockSpec((n_tok, ATT_W), lambda b, s, pt_ref: (b, col))
    rows = H_ATT * n_tok
    pg_rows = PAGE_SIZE * H_ATT
    row_head = jnp.arange(rows, dtype=jnp.int32)[:, None] // n_tok
    col_head = jnp.arange(pg_rows, dtype=jnp.int32)[None, :] % H_ATT
    head_bias = jnp.where(row_head == col_head, 0.0, NEG_BIG).astype(F32)
    grid_spec = pltpu.PrefetchScalarGridSpec(
        num_scalar_prefetch=1,
        grid=(n_batch, n_steps),
        in_specs=[row(COL_QA // ATT_W), row(0), row(0),
                  pl.BlockSpec((rows, pg_rows), lambda b, s, pt_ref: (0, 0))]
        + [page_spec(jj) for jj in range(n_pg)] + [page_spec(jj) for jj in range(n_pg)],
        out_specs=row(0),
        scratch_shapes=[
            pltpu.VMEM((rows, 128), F32),
            pltpu.VMEM((rows, 128), F32),
            pltpu.VMEM((rows, 128), F32),
            pltpu.VMEM((n_past, rows, HD_ATT), F32),
            pltpu.VMEM((128, HD_ATT), F32),
            pltpu.VMEM((128, HD_ATT), F32),
        ],
    )
    return pl.pallas_call(
        functools.partial(_moba_sample_kernel, n_tok=n_tok, n_past=n_past),
        grid_spec=grid_spec,
        out_shape=jax.ShapeDtypeStruct((z.shape[0], ATT_W), z.dtype),
        compiler_params=_params(("arbitrary", "arbitrary")),
        name="moba_sample",
    )(pt, z, k_new, v_new, head_bias, *([ck] * n_pg), *([cv] * n_pg))


MIX_TN = 512


def _gate_mix_kernel(on_ref, oa_ref, wr_ref, wa_ref, gbr_ref, gba_ref, o_ref):
    ret = _dot(on_ref[...].astype(BF16), wr_ref[...])
    att = _dot(oa_ref[...].astype(BF16), wa_ref[...])
    mix = _sigmoid(gbr_ref[...].astype(F32)) * ret + _sigmoid(gba_ref[...].astype(F32)) * att
    o_ref[...] = mix.astype(o_ref.dtype)


def _gate_mix(on, oa, z, w_ret, w_att, tm):
    m = on.shape[0]
    jr = COL_GBR // MIX_TN
    ja = COL_GBA // MIX_TN
    return pl.pallas_call(
        _gate_mix_kernel,
        grid=(m // tm, D_MODEL // MIX_TN),
        in_specs=[
            pl.BlockSpec((tm, RET_V), lambda i, j: (i, 0)),
            pl.BlockSpec((tm, ATT_W), lambda i, j: (i, 0)),
            pl.BlockSpec((RET_V, MIX_TN), lambda i, j: (0, j)),
            pl.BlockSpec((ATT_W, MIX_TN), lambda i, j: (0, j)),
            pl.BlockSpec((tm, MIX_TN), lambda i, j: (i, jr + j)),
            pl.BlockSpec((tm, MIX_TN), lambda i, j: (i, ja + j)),
        ],
        out_specs=pl.BlockSpec((tm, MIX_TN), lambda i, j: (i, j)),
        out_shape=jax.ShapeDtypeStruct((m, D_MODEL), BF16),
        compiler_params=_params(("arbitrary", "arbitrary")),
        name="gate_mix",
    )(on, oa, w_ret, w_att, z, z)


def _out_ln_kernel(mix_ref, w_ref, x_ref, g_ref, b_ref, h_ref, hb_ref):
    mix = _dot(mix_ref[...], w_ref[...])
    h = _layer_norm_rows(ALPHA * x_ref[...] + mix, g_ref[...], b_ref[...])
    h_ref[...] = h
    hb_ref[...] = h.astype(BF16)


def _out_ln(mix_in, w_out, x, g, b, tm):
    m = x.shape[0]
    rowspec = pl.BlockSpec((tm, D_MODEL), lambda i: (i, 0))
    vec = pl.BlockSpec((1, D_MODEL), lambda i: (0, 0))
    return pl.pallas_call(
        _out_ln_kernel,
        grid=(m // tm,),
        in_specs=[rowspec, pl.BlockSpec((D_MODEL, D_MODEL), lambda i: (0, 0)), rowspec, vec, vec],
        out_specs=[rowspec, rowspec],
        out_shape=[jax.ShapeDtypeStruct((m, D_MODEL), F32), jax.ShapeDtypeStruct((m, D_MODEL), BF16)],
        compiler_params=_params(("arbitrary",)),
        name="out_ln",
    )(mix_in, w_out, x, g, b)


MLP_TF = 512


def _mlp_ln_kernel(h_ref, hb_ref, wu_ref, wd_ref, g_ref, b_ref, y_ref, acc_ref):
    f = pl.program_id(1)
    u = _dot(hb_ref[...], wu_ref[...])
    a = jnp.maximum(u, 0.0)
    part = _dot((a * a).astype(BF16), wd_ref[...])

    @pl.when(f == 0)
    def _():
        acc_ref[...] = part

    @pl.when(f > 0)
    def _():
        acc_ref[...] += part

    @pl.when(f == pl.num_programs(1) - 1)
    def _():
        y_ref[...] = _layer_norm_rows(ALPHA * h_ref[...] + acc_ref[...], g_ref[...], b_ref[...])


def _mlp_ln(h, hb, w_up, w_down, g, b, tm):
    m = h.shape[0]
    rowspec = pl.BlockSpec((tm, D_MODEL), lambda i, f: (i, 0))
    vec = pl.BlockSpec((1, D_MODEL), lambda i, f: (0, 0))
    return pl.pallas_call(
        _mlp_ln_kernel,
        grid=(m // tm, D_FF // MLP_TF),
        in_specs=[
            rowspec, rowspec,
            pl.BlockSpec((D_MODEL, MLP_TF), lambda i, f: (0, f)),
            pl.BlockSpec((MLP_TF, D_MODEL), lambda i, f: (f, 0)),
            vec, vec,
        ],
        out_specs=rowspec,
        out_shape=jax.ShapeDtypeStruct((m, D_MODEL), F32),
        scratch_shapes=[pltpu.VMEM((tm, D_MODEL), F32)],
        compiler_params=_params(("arbitrary", "arbitrary")),
        name="mlp_ln",
    )(h, hb, w_up, w_down, g, b)


def _rotary_tables(pos):
    half = DK_RET // 2
    inv = ROPE_BASE ** (-jnp.arange(half, dtype=F32) / half)
    ang = pos.astype(F32)[:, None] * inv[None, :]
    return jnp.cos(ang), jnp.sin(ang)


def _merge(x, z, on, oa, w, tm):
    mix_in = _gate_mix(on, oa, z, w["ret"], w["att"], tm)
    h, hb = _out_ln(mix_in, w["out"], x, w["ln1_g"], w["ln1_b"], tm)
    return _mlp_ln(h, hb, w["up"], w["down"], w["ln2_g"], w["ln2_b"], tm)


def kernel(x_prompt, x_sample, cache_k, cache_v, state_ret, page_table, w_in, ret_gn_gain, w_ret_br,
           w_att_br, w_out, ln1_g, ln1_b, w_up, w_down, ln2_g, ln2_b):
    n_b, seq, _ = x_prompt.shape
    n_db, n_tok, _ = x_sample.shape
    assert w_in.shape[0] == DEPTH == 1
    l = 0
    w = {
        "in": w_in[l].astype(BF16), "ret": w_ret_br[l].astype(BF16), "att": w_att_br[l].astype(BF16),
        "out": w_out[l].astype(BF16), "up": w_up[l].astype(BF16), "down": w_down[l].astype(BF16),
        "ln1_g": ln1_g[l][None, :], "ln1_b": ln1_b[l][None, :],
        "ln2_g": ln2_g[l][None, :], "ln2_b": ln2_b[l][None, :],
    }
    gain = ret_gn_gain[l][:, None, :]

    xp = x_prompt.reshape(n_b * seq, D_MODEL)
    cos_p, sin_p = _rotary_tables(jnp.arange(seq, dtype=jnp.int32))
    tm_p = 1024
    z_p, k_p, v_p = _in_proj(xp, w["in"], cos_p, sin_p, tm_p, BF16)
    on_p, s_p = _retention(z_p, None, _retention_tables(RET_CHUNK), gain, n_b, seq // RET_CHUNK,
                           RET_CHUNK, BF16, n_heads=4)
    oa_p = _moba_prompt(z_p, k_p, v_p, n_b, seq)
    y_p = _merge(xp, z_p, on_p, oa_p, w, 512)

    xs = x_sample.reshape(n_db * n_tok, D_MODEL)
    pos_s = PAST_LEN + jnp.arange(n_tok, dtype=jnp.int32)
    cos_s, sin_s = _rotary_tables(jnp.tile(pos_s, n_db))
    tm_s = n_db * n_tok
    z_s, k_s, v_s = _in_proj(xs, w["in"], cos_s, sin_s, tm_s, F32)
    on_s, s_s = _retention(z_s, state_ret[l], _retention_tables(n_tok), gain, n_db, 1, n_tok, F32,
                           n_heads=H_RET)
    oa_s = _moba_sample(z_s, k_s, v_s, cache_k[l], cache_v[l], page_table, n_db, n_tok)
    y_s = _merge(xs, z_s, on_s, oa_s, w, tm_s)

    return (
        y_p.reshape(n_b, seq, D_MODEL),
        y_s.reshape(n_db, n_tok, D_MODEL),
        k_p.reshape(1, n_b, seq, H_ATT, HD_ATT),
        v_p.reshape(1, n_b, seq, H_ATT, HD_ATT),
        s_p[None],
        k_s.reshape(1, n_db, n_tok, H_ATT, HD_ATT),
        v_s.reshape(1, n_db, n_tok, H_ATT, HD_ATT),
        s_s[None],
    )
```

```python
import functools
import math

import jax
import jax.numpy as jnp
from jax import lax
from jax.experimental import pallas as pl
from jax.experimental.pallas import tpu as pltpu

F32 = jnp.float32
BF16 = jnp.bfloat16

D_MODEL = 2048
DEPTH = 1
PAST_LEN = 8192
PAGE_SIZE = 128
H_RET = 8
DK_RET = 256
DV_RET = 256
RET_CHUNK = 128
H_ATT = 8
HD_ATT = 128
MOBA_BLOCK = 256
MOBA_TOPK = 3
D_FF = 4 * D_MODEL
RET_QK = H_RET * DK_RET
RET_V = H_RET * DV_RET
ATT_W = H_ATT * HD_ATT
W_IN_COLS = 2 * RET_QK + 2 * RET_V + 3 * ATT_W + 2 * D_MODEL
ROPE_BASE = 10000.0
LN_EPS = 1e-5
GN_EPS = 1e-5
ALPHA = (2 * DEPTH) ** 0.25

COL_QR = 0
COL_KR = RET_QK
COL_VR = 2 * RET_QK
COL_GR = 2 * RET_QK + RET_V
COL_QA = 2 * RET_QK + 2 * RET_V
COL_KA = COL_QA + ATT_W
COL_VA = COL_KA + ATT_W
COL_GBR = COL_VA + ATT_W
COL_GBA = COL_GBR + D_MODEL

NEG_BIG = -1e30
SOFTMAX_C = HD_ATT ** -0.5 * math.log2(math.e)
PAGES_PER_BLOCK = MOBA_BLOCK // PAGE_SIZE
VMEM_LIMIT = 56 * 1024 * 1024


def _params(semantics):
    return pltpu.CompilerParams(dimension_semantics=semantics, vmem_limit_bytes=VMEM_LIMIT)


def _sigmoid(x):
    return 1.0 / (1.0 + jnp.exp(-x))


def _dot(a, b):
    return jnp.dot(a, b, preferred_element_type=F32)


def _dot_nt(a, b):
    return lax.dot_general(a, b, (((1,), (1,)), ((), ())), preferred_element_type=F32)


def _dot_tn(a, b):
    return lax.dot_general(a, b, (((0,), (0,)), ((), ())), preferred_element_type=F32)


def _layer_norm_rows(x, g, b):
    mu = jnp.mean(x, axis=-1, keepdims=True)
    xc = x - mu
    var = jnp.mean(xc * xc, axis=-1, keepdims=True)
    return xc * lax.rsqrt(var + LN_EPS) * g + b


IN_TN = 512
IN_STEP = 1024
_J_ROT_END = COL_VR // IN_STEP
_J_KR = COL_KR // IN_STEP
_J_KA = COL_KA // IN_STEP
_J_VA = COL_VA // IN_STEP
assert ATT_W == IN_STEP and COL_VR % IN_STEP == 0 and COL_QA % IN_STEP == 0


def _in_proj_kernel(x_ref, w_ref, cos_ref, sin_ref, z_ref, k_ref, v_ref, xb_ref):
    j = pl.program_id(1)
    tiles = [slice(t * IN_TN, (t + 1) * IN_TN) for t in range(IN_STEP // IN_TN)]

    @pl.when(j == 0)
    def _():
        xb_ref[...] = x_ref[...].astype(BF16)

    def tile_dot(sl):
        return _dot(xb_ref[...], w_ref[:, sl])

    @pl.when(j < _J_ROT_END)
    def _():
        cos = cos_ref[...]
        sin = sin_ref[...]
        scale = jnp.where(j >= _J_KR, DK_RET ** -0.5, 1.0).astype(F32)
        half = DK_RET // 2
        for sl in tiles:
            acc = tile_dot(sl)
            for hh in range(IN_TN // DK_RET):
                c0 = hh * DK_RET
                o0 = sl.start + c0
                x1 = acc[:, c0:c0 + half]
                x2 = acc[:, c0 + half:c0 + DK_RET]
                z_ref[:, o0:o0 + half] = ((x1 * cos - x2 * sin) * scale).astype(z_ref.dtype)
                z_ref[:, o0 + half:o0 + DK_RET] = ((x1 * sin + x2 * cos) * scale).astype(z_ref.dtype)

    @pl.when((j >= _J_ROT_END) & (j != _J_KA) & (j != _J_VA))
    def _():
        for sl in tiles:
            z_ref[:, sl] = tile_dot(sl).astype(z_ref.dtype)

    def new_rows(out_ref):
        for sl in tiles:
            acc = tile_dot(sl)
            z_ref[:, sl] = acc.astype(z_ref.dtype)
            out_ref[:, sl] = acc

    pl.when(j == _J_KA)(lambda: new_rows(k_ref))
    pl.when(j == _J_VA)(lambda: new_rows(v_ref))


def _in_proj(x, w_bf, cos_t, sin_t, tm, z_dtype):
    m = x.shape[0]
    n_tab = cos_t.shape[0] // tm
    return pl.pallas_call(
        _in_proj_kernel,
        grid=(m // tm, W_IN_COLS // IN_STEP),
        in_specs=[
            pl.BlockSpec((tm, D_MODEL), lambda i, j: (i, 0)),
            pl.BlockSpec((D_MODEL, IN_STEP), lambda i, j: (0, j)),
            pl.BlockSpec((tm, DK_RET // 2), lambda i, j: (i % n_tab, 0)),
            pl.BlockSpec((tm, DK_RET // 2), lambda i, j: (i % n_tab, 0)),
        ],
        out_specs=[
            pl.BlockSpec((tm, IN_STEP), lambda i, j: (i, j)),
            pl.BlockSpec((tm, ATT_W), lambda i, j: (i, 0)),
            pl.BlockSpec((tm, ATT_W), lambda i, j: (i, 0)),
        ],
        out_shape=[
            jax.ShapeDtypeStruct((m, W_IN_COLS), z_dtype),
            jax.ShapeDtypeStruct((m, ATT_W), F32),
            jax.ShapeDtypeStruct((m, ATT_W), F32),
        ],
        scratch_shapes=[pltpu.VMEM((tm, D_MODEL), BF16)],
        compiler_params=_params(("arbitrary", "arbitrary")),
        name="in_proj",
    )(x, w_bf, cos_t, sin_t)


def _retention_kernel(*refs, c_data, has_init, n_heads):
    if has_init:
        (q_ref, k_ref, v_ref, g_ref, s0_ref, dec_ref, qd_ref, kd_ref, gc_ref, gain_ref,
         on_ref, s_ref, pad_ref) = refs
    else:
        (q_ref, k_ref, v_ref, g_ref, dec_ref, qd_ref, kd_ref, gc_ref, gain_ref,
         on_ref, s_ref, pad_ref) = refs
        s0_ref = None
    c = pl.program_id(2)
    padded = c_data != RET_CHUNK

    @pl.when(c == 0)
    def _():
        if has_init:
            s_ref[...] = s0_ref[...]
        else:
            s_ref[...] = jnp.zeros(s_ref.shape, F32)

    def load(ref, slot):
        if not padded:
            return ref[...]
        pad_ref[slot] = jnp.zeros(pad_ref.shape[1:], F32)
        pad_ref[slot, 0:c_data, :] = ref[...].astype(F32)
        return pad_ref[slot]

    q_all = load(q_ref, 0)
    k_all = load(k_ref, 1)
    v_all = load(v_ref, 2)
    g_all = g_ref[...]

    for hh in range(n_heads):
        lo, hi = hh * DK_RET, (hh + 1) * DK_RET
        q = q_all[:, lo:hi].astype(BF16)
        k = k_all[:, lo:hi]
        v = v_all[:, lo:hi].astype(BF16)
        s_old = s_ref[0, hh]
        scores = _dot_nt(q, k.astype(BF16)) * dec_ref[hh]
        inner = _dot(scores.astype(BF16), v)
        cross = _dot(q, s_old.astype(BF16)) * qd_ref[hh]
        o = inner + cross
        kd = (k.astype(F32) * kd_ref[hh]).astype(BF16)
        s_ref[0, hh] = gc_ref[hh] * s_old + _dot_tn(kd, v)
        if padded:
            o = o[0:c_data]
        mu = jnp.mean(o, axis=-1, keepdims=True)
        oc = o - mu
        var = jnp.mean(oc * oc, axis=-1, keepdims=True)
        o_n = oc * lax.rsqrt(var + GN_EPS) * gain_ref[hh]
        g = g_all[:, lo:hi].astype(F32)
        on_ref[:, lo:hi] = (o_n * (g * _sigmoid(g))).astype(on_ref.dtype)


def _retention(z, s0, tabs, gain, n_batch, n_chunks, c_data, out_dtype, n_heads):
    dec, qd, kd, gc = tabs
    m = z.shape[0]
    wblk = DK_RET * n_heads
    has_init = s0 is not None

    def zspec(col0):
        base = col0 // wblk
        return pl.BlockSpec((c_data, wblk), lambda b, hg, c: (b * n_chunks + c, base + hg))

    tab3 = lambda shape: pl.BlockSpec((n_heads,) + shape, lambda b, hg, c: (hg, 0, 0))
    sspec = pl.BlockSpec((1, n_heads, DK_RET, DV_RET), lambda b, hg, c: (b, hg, 0, 0))
    in_specs = [zspec(COL_QR), zspec(COL_KR), zspec(COL_VR), zspec(COL_GR)]
    args = [z, z, z, z]
    if has_init:
        in_specs.append(sspec)
        args.append(s0)
    in_specs += [tab3((RET_CHUNK, RET_CHUNK)), tab3((RET_CHUNK, 1)), tab3((RET_CHUNK, 1)),
                 tab3((1, 1)), tab3((1, DV_RET))]
    args += [dec, qd, kd, gc, gain]
    return pl.pallas_call(
        functools.partial(_retention_kernel, c_data=c_data, has_init=has_init, n_heads=n_heads),
        grid=(n_batch, H_RET // n_heads, n_chunks),
        in_specs=in_specs,
        out_specs=[
            pl.BlockSpec((c_data, wblk), lambda b, hg, c: (b * n_chunks + c, hg)),
            sspec,
        ],
        out_shape=[
            jax.ShapeDtypeStruct((m, RET_V), out_dtype),
            jax.ShapeDtypeStruct((n_batch, H_RET, DK_RET, DV_RET), F32),
        ],
        scratch_shapes=[pltpu.VMEM((3, RET_CHUNK, wblk), F32)],
        compiler_params=_params(("arbitrary", "arbitrary", "arbitrary")),
        name="retention",
    )(*args)


def _retention_tables(c_data):
    log_gamma = jnp.log1p(-jnp.exp2(-5.0 - jnp.arange(H_RET, dtype=F32)))
    idx = jnp.arange(RET_CHUNK, dtype=F32)
    live = idx < c_data
    diff = idx[:, None] - idx[None, :]
    causal = (diff >= 0) & live[:, None] & live[None, :]
    dec = jnp.where(causal, jnp.exp(log_gamma[:, None, None] * jnp.where(causal, diff, 0.0)), 0.0)
    qd = jnp.exp(log_gamma[:, None] * (idx[None, :] + 1.0))
    kd = jnp.where(live[None, :], jnp.exp(log_gamma[:, None] * (c_data - 1.0 - idx[None, :])), 0.0)
    gc = jnp.exp(log_gamma * c_data)
    return dec, qd[:, :, None], kd[:, :, None], gc[:, None, None]


def _moba_prompt_kernel(q_ref, k_ref, v_ref, o_ref, kb_ref, vb_ref, km_ref):
    nb = k_ref.shape[0] // MOBA_BLOCK
    blk = MOBA_BLOCK

    kb_ref[...] = k_ref[...].astype(BF16)
    vb_ref[...] = v_ref[...].astype(BF16)
    km_ref[...] = jnp.zeros(km_ref.shape, F32)
    for n in range(nb):
        km_ref[n:n + 1, :] = jnp.sum(k_ref[n * blk:(n + 1) * blk, :], axis=0, keepdims=True) * (1.0 / blk)
    kmb = km_ref[...].astype(BF16)

    lane = lax.broadcasted_iota(jnp.int32, (blk, 128), 1)
    row = lax.broadcasted_iota(jnp.int32, (blk, blk), 0)
    col = lax.broadcasted_iota(jnp.int32, (blk, blk), 1)
    causal_bias = jnp.where(col <= row, 0.0, NEG_BIG).astype(F32)

    for i in range(nb):
        q = q_ref[i * blk:(i + 1) * blk, :].astype(BF16)
        pieces = []
        if i > 0:
            s_el = jnp.where(lane < i, _dot_nt(q, kmb), -jnp.inf)
            cnt = jnp.zeros(s_el.shape, jnp.int32)
            for n in range(i):
                cn = s_el[:, n:n + 1]
                ahead = (cn > s_el) | ((cn == s_el) & (lane > n))
                cnt = cnt + ahead.astype(jnp.int32)
            bias = jnp.where((lane < i) & (cnt < MOBA_TOPK), 0.0, NEG_BIG).astype(F32)
            pieces = [jnp.broadcast_to(bias[:, n:n + 1], (blk, blk)) for n in range(i)]
        nk = (i + 1) * blk
        s = _dot_nt(q, kb_ref[0:nk, :]) + jnp.concatenate(pieces + [causal_bias], axis=1)
        m = jnp.max(s, axis=-1, keepdims=True)
        p = jnp.exp2((s - m) * SOFTMAX_C)
        l = jnp.sum(p, axis=-1, keepdims=True)
        o = _dot(p.astype(BF16), vb_ref[0:nk, :]) / l
        o_ref[i * blk:(i + 1) * blk, :] = o.astype(o_ref.dtype)


def _moba_prompt(z, k_new, v_new, n_batch, seq):
    m = z.shape[0]
    qcol = COL_QA // HD_ATT
    head_rows = lambda col0: pl.BlockSpec((seq, HD_ATT), lambda b, h: (b, col0 + h))
    return pl.pallas_call(
        _moba_prompt_kernel,
        grid=(n_batch, H_ATT),
        in_specs=[head_rows(qcol), head_rows(0), head_rows(0)],
        out_specs=head_rows(0),
        out_shape=jax.ShapeDtypeStruct((m, ATT_W), z.dtype),
        scratch_shapes=[
            pltpu.VMEM((seq, HD_ATT), BF16),
            pltpu.VMEM((seq, HD_ATT), BF16),
            pltpu.VMEM((128, HD_ATT), F32),
        ],
        compiler_params=_params(("arbitrary", "arbitrary")),
        name="moba_prompt",
    )(z, k_new, v_new)


SMP_G = 4


def _moba_sample_kernel(pt_ref, q_ref, kn_ref, vn_ref, hb_ref, *rest, n_tok, n_past):
    n_pg = SMP_G * PAGES_PER_BLOCK
    ck = rest[:n_pg]
    cv = rest[n_pg:2 * n_pg]
    o_ref, m_all, l_all, s_all, o_sc, kpad, vpad = rest[2 * n_pg:]
    s_id = pl.program_id(1)
    rows = H_ATT * n_tok
    tok_shift = n_tok.bit_length() - 1
    assert n_tok == 1 << tok_shift and H_ATT & (H_ATT - 1) == 0 and rows <= 128

    @pl.when(s_id == 0)
    def _():
        m_all[...] = jnp.zeros(m_all.shape, F32)
        l_all[...] = jnp.zeros(l_all.shape, F32)
        s_all[...] = jnp.zeros(s_all.shape, F32)

    def by_head(ref):
        x = ref[...]
        return jnp.concatenate([x[:, h * HD_ATT:(h + 1) * HD_ATT] for h in range(H_ATT)], axis=0)

    qb = by_head(q_ref).astype(BF16)
    qr = qb.astype(F32)
    hb = hb_ref[...]
    lane = lax.broadcasted_iota(jnp.int32, (rows, 128), 1)

    for g in range(SMP_G):
        n = s_id * SMP_G + g
        k0 = ck[2 * g][...]
        k1 = ck[2 * g + 1][...]
        ksum = (jnp.sum(k0.reshape(PAGE_SIZE, H_ATT, HD_ATT), axis=0)
                + jnp.sum(k1.reshape(PAGE_SIZE, H_ATT, HD_ATT), axis=0))
        kmean = (ksum * (1.0 / MOBA_BLOCK)).astype(BF16).astype(F32)
        kexp = jnp.concatenate(
            [jnp.broadcast_to(kmean[h:h + 1, :], (n_tok, HD_ATT)) for h in range(H_ATT)], axis=0)
        sc = jnp.sum(qr * kexp, axis=-1, keepdims=True)
        s0 = _dot_nt(qb, k0.astype(BF16)) + hb
        s1 = _dot_nt(qb, k1.astype(BF16)) + hb
        mh = jnp.maximum(jnp.max(s0, axis=-1, keepdims=True), jnp.max(s1, axis=-1, keepdims=True))
        p0 = jnp.exp2((s0 - mh) * SOFTMAX_C)
        p1 = jnp.exp2((s1 - mh) * SOFTMAX_C)
        lh = jnp.sum(p0, axis=-1, keepdims=True) + jnp.sum(p1, axis=-1, keepdims=True)
        o_sc[n] = (_dot(p0.astype(BF16), cv[2 * g][...].astype(BF16))
                   + _dot(p1.astype(BF16), cv[2 * g + 1][...].astype(BF16)))
        here = lane == n
        m_all[...] = jnp.where(here, mh, m_all[...])
        l_all[...] = jnp.where(here, lh, l_all[...])
        s_all[...] = jnp.where(here, sc, s_all[...])

    @pl.when(s_id == pl.num_programs(1) - 1)
    def _():
        past = lane < n_past
        sc_all = jnp.where(past, s_all[...], -jnp.inf)
        cnt = jnp.zeros(sc_all.shape, jnp.int32)
        for nn in range(n_past):
            col = sc_all[:, nn:nn + 1]
            ahead = (col > sc_all) | ((col == sc_all) & (lane > nn))
            cnt = cnt + ahead.astype(jnp.int32)
        sel = past & (cnt < MOBA_TOPK)
        m_sel = jnp.where(sel, m_all[...], NEG_BIG)

        kpad[...] = jnp.zeros(kpad.shape, F32)
        vpad[...] = jnp.zeros(vpad.shape, F32)
        kpad[0:rows, :] = by_head(kn_ref)
        vpad[0:rows, :] = by_head(vn_ref)
        r_id = lax.broadcasted_iota(jnp.int32, (rows, 128), 0)
        own_ok = ((lane < rows) & ((lane >> tok_shift) == (r_id >> tok_shift))
                  & ((lane & (n_tok - 1)) <= (r_id & (n_tok - 1))))
        s_own = jnp.where(own_ok, _dot_nt(qb, kpad[...].astype(BF16)), NEG_BIG)
        big = jnp.maximum(jnp.max(m_sel, axis=-1, keepdims=True), jnp.max(s_own, axis=-1, keepdims=True))
        w = jnp.where(sel, jnp.exp2((m_sel - big) * SOFTMAX_C), 0.0)
        p_own = jnp.exp2((s_own - big) * SOFTMAX_C)
        den = jnp.sum(w * l_all[...], axis=-1, keepdims=True) + jnp.sum(p_own, axis=-1, keepdims=True)
        num = _dot(p_own.astype(BF16), vpad[...].astype(BF16))
        for nn in range(n_past):
            num = num + w[:, nn:nn + 1] * o_sc[nn]
        out = num / den
        for h in range(H_ATT):
            o_ref[:, h * HD_ATT:(h + 1) * HD_ATT] = out[h * n_tok:(h + 1) * n_tok, :].astype(o_ref.dtype)


def _moba_sample(z, k_new, v_new, cache_k, cache_v, page_table, n_batch, n_tok):
    n_pages = page_table.shape[1]
    n_past = n_pages // PAGES_PER_BLOCK
    n_pg = SMP_G * PAGES_PER_BLOCK
    n_steps = n_past // SMP_G
    pt = page_table.reshape(-1)
    ck = cache_k.reshape(cache_k.shape[0], PAGE_SIZE * H_ATT, HD_ATT)
    cv = cache_v.reshape(cache_v.shape[0], PAGE_SIZE * H_ATT, HD_ATT)

    def page_spec(jj):
        return pl.BlockSpec((None, PAGE_SIZE * H_ATT, HD_ATT),
                            lambda b, s, pt_ref: (pt_ref[b * n_pages + s * n_pg + jj], 0, 0))

    row = lambda col: pl.BlockSpec((n_tok, ATT_W), lambda b, s, pt_ref: (b, col))
    rows = H_ATT * n_tok
    pg_rows = PAGE_SIZE * H_ATT
    row_head = jnp.arange(rows, dtype=jnp.int32)[:, None] // n_tok
    col_head = jnp.arange(pg_rows, dtype=jnp.int32)[None, :] % H_ATT
    head_bias = jnp.where(row_head == col_head, 0.0, NEG_BIG).astype(F32)
    grid_spec = pltpu.PrefetchScalarGridSpec(
        num_scalar_prefetch=1,
        grid=(n_batch, n_steps),
        in_specs=[row(COL_QA // ATT_W), row(0), row(0),
                  pl.BlockSpec((rows, pg_rows), lambda b, s, pt_ref: (0, 0))]
        + [page_spec(jj) for jj in range(n_pg)] + [page_spec(jj) for jj in range(n_pg)],
        out_specs=row(0),
        scratch_shapes=[
            pltpu.VMEM((rows, 128), F32),
            pltpu.VMEM((rows, 128), F32),
            pltpu.VMEM((rows, 128), F32),
            pltpu.VMEM((n_past, rows, HD_ATT), F32),
            pltpu.VMEM((128, HD_ATT), F32),
            pltpu.VMEM((128, HD_ATT), F32),
        ],
    )
    return pl.pallas_call(
        functools.partial(_moba_sample_kernel, n_tok=n_tok, n_past=n_past),
        grid_spec=grid_spec,
        out_shape=jax.ShapeDtypeStruct((z.shape[0], ATT_W), z.dtype),
        compiler_params=_params(("arbitrary", "arbitrary")),
        name="moba_sample",
    )(pt, z, k_new, v_new, head_bias, *([ck] * n_pg), *([cv] * n_pg))


MIX_TN = 1024
DOT_TN = 512


def _gate_mix_kernel(on_ref, oa_ref, wr_ref, wa_ref, gbr_ref, gba_ref, o_ref):
    on = on_ref[...].astype(BF16)
    oa = oa_ref[...].astype(BF16)
    for t in range(MIX_TN // DOT_TN):
        sl = slice(t * DOT_TN, (t + 1) * DOT_TN)
        ret = _dot(on, wr_ref[:, sl])
        att = _dot(oa, wa_ref[:, sl])
        mix = _sigmoid(gbr_ref[:, sl].astype(F32)) * ret + _sigmoid(gba_ref[:, sl].astype(F32)) * att
        o_ref[:, sl] = mix.astype(o_ref.dtype)


def _gate_mix(on, oa, z, w_ret, w_att, tm):
    m = on.shape[0]
    jr = COL_GBR // MIX_TN
    ja = COL_GBA // MIX_TN
    return pl.pallas_call(
        _gate_mix_kernel,
        grid=(m // tm, D_MODEL // MIX_TN),
        in_specs=[
            pl.BlockSpec((tm, RET_V), lambda i, j: (i, 0)),
            pl.BlockSpec((tm, ATT_W), lambda i, j: (i, 0)),
            pl.BlockSpec((RET_V, MIX_TN), lambda i, j: (0, j)),
            pl.BlockSpec((ATT_W, MIX_TN), lambda i, j: (0, j)),
            pl.BlockSpec((tm, MIX_TN), lambda i, j: (i, jr + j)),
            pl.BlockSpec((tm, MIX_TN), lambda i, j: (i, ja + j)),
        ],
        out_specs=pl.BlockSpec((tm, MIX_TN), lambda i, j: (i, j)),
        out_shape=jax.ShapeDtypeStruct((m, D_MODEL), BF16),
        compiler_params=_params(("arbitrary", "arbitrary")),
        name="gate_mix",
    )(on, oa, w_ret, w_att, z, z)


OUT_ROWS = 256


def _out_ln_kernel(mix_ref, w_ref, x_ref, g_ref, b_ref, h_ref, hb_ref):
    for r in range(mix_ref.shape[0] // OUT_ROWS):
        rs = slice(r * OUT_ROWS, (r + 1) * OUT_ROWS)
        mix = _dot(mix_ref[rs, :], w_ref[...])
        h = _layer_norm_rows(ALPHA * x_ref[rs, :] + mix, g_ref[...], b_ref[...])
        h_ref[rs, :] = h
        hb_ref[rs, :] = h.astype(BF16)


def _out_ln(mix_in, w_out, x, g, b, tm):
    m = x.shape[0]
    rowspec = pl.BlockSpec((tm, D_MODEL), lambda i: (i, 0))
    vec = pl.BlockSpec((1, D_MODEL), lambda i: (0, 0))
    return pl.pallas_call(
        _out_ln_kernel,
        grid=(m // tm,),
        in_specs=[rowspec, pl.BlockSpec((D_MODEL, D_MODEL), lambda i: (0, 0)), rowspec, vec, vec],
        out_specs=[rowspec, rowspec],
        out_shape=[jax.ShapeDtypeStruct((m, D_MODEL), F32), jax.ShapeDtypeStruct((m, D_MODEL), BF16)],
        compiler_params=_params(("arbitrary",)),
        name="out_ln",
    )(mix_in, w_out, x, g, b)


MLP_TF = 1024


def _mlp_ln_kernel(h_ref, hb_ref, wu_ref, wd_ref, g_ref, b_ref, y_ref, acc_ref):
    f = pl.program_id(1)

    @pl.when(f == 0)
    def _():
        acc_ref[...] = ALPHA * h_ref[...]

    hb = hb_ref[...]
    pieces = []
    for c in range(MLP_TF // DOT_TN):
        a = jnp.maximum(_dot(hb, wu_ref[:, c * DOT_TN:(c + 1) * DOT_TN]), 0.0)
        pieces.append((a * a).astype(BF16))
    act = jnp.concatenate(pieces, axis=1)
    for n in range(D_MODEL // DOT_TN):
        sl = slice(n * DOT_TN, (n + 1) * DOT_TN)
        acc_ref[:, sl] += _dot(act, wd_ref[:, sl])

    @pl.when(f == pl.num_programs(1) - 1)
    def _():
        y_ref[...] = _layer_norm_rows(acc_ref[...], g_ref[...], b_ref[...])


def _mlp_ln(h, hb, w_up, w_down, g, b, tm):
    m = h.shape[0]
    rowspec = pl.BlockSpec((tm, D_MODEL), lambda i, f: (i, 0))
    vec = pl.BlockSpec((1, D_MODEL), lambda i, f: (0, 0))
    return pl.pallas_call(
        _mlp_ln_kernel,
        grid=(m // tm, D_FF // MLP_TF),
        in_specs=[
            rowspec, rowspec,
            pl.BlockSpec((D_MODEL, MLP_TF), lambda i, f: (0, f)),
            pl.BlockSpec((MLP_TF, D_MODEL), lambda i, f: (f, 0)),
            vec, vec,
        ],
        out_specs=rowspec,
        out_shape=jax.ShapeDtypeStruct((m, D_MODEL), F32),
        scratch_shapes=[pltpu.VMEM((tm, D_MODEL), F32)],
        compiler_params=_params(("arbitrary", "arbitrary")),
        name="mlp_ln",
    )(h, hb, w_up, w_down, g, b)


def _rotary_tables(pos):
    half = DK_RET // 2
    inv = ROPE_BASE ** (-jnp.arange(half, dtype=F32) / half)
    ang = pos.astype(F32)[:, None] * inv[None, :]
    return jnp.cos(ang), jnp.sin(ang)


def _merge(x, z, on, oa, w, tm):
    mix_in = _gate_mix(on, oa, z, w["ret"], w["att"], tm)
    h, hb = _out_ln(mix_in, w["out"], x, w["ln1_g"], w["ln1_b"], tm)
    return _mlp_ln(h, hb, w["up"], w["down"], w["ln2_g"], w["ln2_b"], tm)


def kernel(x_prompt, x_sample, cache_k, cache_v, state_ret, page_table, w_in, ret_gn_gain, w_ret_br,
           w_att_br, w_out, ln1_g, ln1_b, w_up, w_down, ln2_g, ln2_b):
    n_b, seq, _ = x_prompt.shape
    n_db, n_tok, _ = x_sample.shape
    assert w_in.shape[0] == DEPTH == 1
    l = 0
    w = {
        "in": w_in[l].astype(BF16), "ret": w_ret_br[l].astype(BF16), "att": w_att_br[l].astype(BF16),
        "out": w_out[l].astype(BF16), "up": w_up[l].astype(BF16), "down": w_down[l].astype(BF16),
        "ln1_g": ln1_g[l][None, :], "ln1_b": ln1_b[l][None, :],
        "ln2_g": ln2_g[l][None, :], "ln2_b": ln2_b[l][None, :],
    }
    gain = ret_gn_gain[l][:, None, :]

    xp = x_prompt.reshape(n_b * seq, D_MODEL)
    cos_p, sin_p = _rotary_tables(jnp.arange(seq, dtype=jnp.int32))
    tm_p = 512
    z_p, k_p, v_p = _in_proj(xp, w["in"], cos_p, sin_p, tm_p, BF16)
    on_p, s_p = _retention(z_p, None, _retention_tables(RET_CHUNK), gain, n_b, seq // RET_CHUNK,
                           RET_CHUNK, BF16, n_heads=4)
    oa_p = _moba_prompt(z_p, k_p, v_p, n_b, seq)
    y_p = _merge(xp, z_p, on_p, oa_p, w, 512)

    xs = x_sample.reshape(n_db * n_tok, D_MODEL)
    pos_s = PAST_LEN + jnp.arange(n_tok, dtype=jnp.int32)
    cos_s, sin_s = _rotary_tables(jnp.tile(pos_s, n_db))
    tm_s = n_db * n_tok
    z_s, k_s, v_s = _in_proj(xs, w["in"], cos_s, sin_s, tm_s, F32)
    on_s, s_s = _retention(z_s, state_ret[l], _retention_tables(n_tok), gain, n_db, 1, n_tok, F32,
                           n_heads=H_RET)
    oa_s = _moba_sample(z_s, k_s, v_s, cache_k[l], cache_v[l], page_table, n_db, n_tok)
    y_s = _merge(xs, z_s, on_s, oa_s, w, tm_s)

    return (
        y_p.reshape(n_b, seq, D_MODEL),
        y_s.reshape(n_db, n_tok, D_MODEL),
        k_p.reshape(1, n_b, seq, H_ATT, HD_ATT),
        v_p.reshape(1, n_b, seq, H_ATT, HD_ATT),
        s_p[None],
        k_s.reshape(1, n_db, n_tok, H_ATT, HD_ATT),
        v_s.reshape(1, n_db, n_tok, H_ATT, HD_ATT),
        s_s[None],
    )
```

```python
import functools
import math

import jax
import jax.numpy as jnp
from jax import lax
from jax.experimental import pallas as pl
from jax.experimental.pallas import tpu as pltpu

F32 = jnp.float32
BF16 = jnp.bfloat16

D_MODEL = 2048
DEPTH = 1
PAST_LEN = 8192
PAGE_SIZE = 128
H_RET = 8
DK_RET = 256
DV_RET = 256
RET_CHUNK = 128
H_ATT = 8
HD_ATT = 128
MOBA_BLOCK = 256
MOBA_TOPK = 3
D_FF = 4 * D_MODEL
RET_QK = H_RET * DK_RET
RET_V = H_RET * DV_RET
ATT_W = H_ATT * HD_ATT
W_IN_COLS = 2 * RET_QK + 2 * RET_V + 3 * ATT_W + 2 * D_MODEL
ROPE_BASE = 10000.0
LN_EPS = 1e-5
GN_EPS = 1e-5
ALPHA = (2 * DEPTH) ** 0.25

COL_QR = 0
COL_KR = RET_QK
COL_VR = 2 * RET_QK
COL_GR = 2 * RET_QK + RET_V
COL_QA = 2 * RET_QK + 2 * RET_V
COL_KA = COL_QA + ATT_W
COL_VA = COL_KA + ATT_W
COL_GBR = COL_VA + ATT_W
COL_GBA = COL_GBR + D_MODEL

NEG_BIG = -1e30
SOFTMAX_C = HD_ATT ** -0.5 * math.log2(math.e)
PAGES_PER_BLOCK = MOBA_BLOCK // PAGE_SIZE
VMEM_LIMIT = 56 * 1024 * 1024


def _params(semantics):
    return pltpu.CompilerParams(dimension_semantics=semantics, vmem_limit_bytes=VMEM_LIMIT)


def _sigmoid(x):
    return 1.0 / (1.0 + jnp.exp(-x))


def _dot(a, b):
    return jnp.dot(a, b, preferred_element_type=F32)


def _dot_nt(a, b):
    return lax.dot_general(a, b, (((1,), (1,)), ((), ())), preferred_element_type=F32)


def _dot_tn(a, b):
    return lax.dot_general(a, b, (((0,), (0,)), ((), ())), preferred_element_type=F32)


def _layer_norm_rows(x, g, b):
    mu = jnp.mean(x, axis=-1, keepdims=True)
    xc = x - mu
    var = jnp.mean(xc * xc, axis=-1, keepdims=True)
    return xc * lax.rsqrt(var + LN_EPS) * g + b


IN_TN = 512
IN_STEP = 1024
_J_ROT_END = COL_VR // IN_STEP
_J_KR = COL_KR // IN_STEP
_J_KA = COL_KA // IN_STEP
_J_VA = COL_VA // IN_STEP
assert ATT_W == IN_STEP and COL_VR % IN_STEP == 0 and COL_QA % IN_STEP == 0


def _in_proj_kernel(x_ref, w_ref, cos_ref, sin_ref, z_ref, k_ref, v_ref, xb_ref):
    j = pl.program_id(1)
    tiles = [slice(t * IN_TN, (t + 1) * IN_TN) for t in range(IN_STEP // IN_TN)]

    @pl.when(j == 0)
    def _():
        xb_ref[...] = x_ref[...].astype(BF16)

    def tile_dot(sl):
        return _dot(xb_ref[...], w_ref[:, sl])

    @pl.when(j < _J_ROT_END)
    def _():
        cos = cos_ref[...]
        sin = sin_ref[...]
        scale = jnp.where(j >= _J_KR, DK_RET ** -0.5, 1.0).astype(F32)
        half = DK_RET // 2
        for sl in tiles:
            acc = tile_dot(sl)
            for hh in range(IN_TN // DK_RET):
                c0 = hh * DK_RET
                o0 = sl.start + c0
                x1 = acc[:, c0:c0 + half]
                x2 = acc[:, c0 + half:c0 + DK_RET]
                z_ref[:, o0:o0 + half] = ((x1 * cos - x2 * sin) * scale).astype(z_ref.dtype)
                z_ref[:, o0 + half:o0 + DK_RET] = ((x1 * sin + x2 * cos) * scale).astype(z_ref.dtype)

    @pl.when((j >= _J_ROT_END) & (j != _J_KA) & (j != _J_VA))
    def _():
        for sl in tiles:
            z_ref[:, sl] = tile_dot(sl).astype(z_ref.dtype)

    def new_rows(out_ref):
        for sl in tiles:
            acc = tile_dot(sl)
            z_ref[:, sl] = acc.astype(z_ref.dtype)
            out_ref[:, sl] = acc

    pl.when(j == _J_KA)(lambda: new_rows(k_ref))
    pl.when(j == _J_VA)(lambda: new_rows(v_ref))


def _in_proj(x, w_bf, cos_t, sin_t, tm, z_dtype):
    m = x.shape[0]
    n_tab = cos_t.shape[0] // tm
    return pl.pallas_call(
        _in_proj_kernel,
        grid=(m // tm, W_IN_COLS // IN_STEP),
        in_specs=[
            pl.BlockSpec((tm, D_MODEL), lambda i, j: (i, 0), pipeline_mode=pl.Buffered(1)),
            pl.BlockSpec((D_MODEL, IN_STEP), lambda i, j: (0, j)),
            pl.BlockSpec((tm, DK_RET // 2), lambda i, j: (i % n_tab, 0)),
            pl.BlockSpec((tm, DK_RET // 2), lambda i, j: (i % n_tab, 0)),
        ],
        out_specs=[
            pl.BlockSpec((tm, IN_STEP), lambda i, j: (i, j)),
            pl.BlockSpec((tm, ATT_W), lambda i, j: (i, 0)),
            pl.BlockSpec((tm, ATT_W), lambda i, j: (i, 0)),
        ],
        out_shape=[
            jax.ShapeDtypeStruct((m, W_IN_COLS), z_dtype),
            jax.ShapeDtypeStruct((m, ATT_W), F32),
            jax.ShapeDtypeStruct((m, ATT_W), F32),
        ],
        scratch_shapes=[pltpu.VMEM((tm, D_MODEL), BF16)],
        compiler_params=_params(("arbitrary", "arbitrary")),
        name="in_proj",
    )(x, w_bf, cos_t, sin_t)


def _retention_kernel(*refs, c_data, has_init, n_heads):
    if has_init:
        (q_ref, k_ref, v_ref, g_ref, s0_ref, dec_ref, qd_ref, kd_ref, gc_ref, gain_ref,
         on_ref, s_ref, pad_ref) = refs
    else:
        (q_ref, k_ref, v_ref, g_ref, dec_ref, qd_ref, kd_ref, gc_ref, gain_ref,
         on_ref, s_ref, pad_ref) = refs
        s0_ref = None
    c = pl.program_id(2)
    padded = c_data != RET_CHUNK

    @pl.when(c == 0)
    def _():
        if has_init:
            s_ref[...] = s0_ref[...]
        else:
            s_ref[...] = jnp.zeros(s_ref.shape, F32)

    def load(ref, slot):
        if not padded:
            return ref[...]
        pad_ref[slot] = jnp.zeros(pad_ref.shape[1:], F32)
        pad_ref[slot, 0:c_data, :] = ref[...].astype(F32)
        return pad_ref[slot]

    q_all = load(q_ref, 0)
    k_all = load(k_ref, 1)
    v_all = load(v_ref, 2)
    g_all = g_ref[...]

    for hh in range(n_heads):
        lo, hi = hh * DK_RET, (hh + 1) * DK_RET
        q = q_all[:, lo:hi].astype(BF16)
        k = k_all[:, lo:hi]
        v = v_all[:, lo:hi].astype(BF16)
        s_old = s_ref[0, hh]
        scores = _dot_nt(q, k.astype(BF16)) * dec_ref[hh]
        inner = _dot(scores.astype(BF16), v)
        cross = _dot(q, s_old.astype(BF16)) * qd_ref[hh]
        o = inner + cross
        kd = (k.astype(F32) * kd_ref[hh]).astype(BF16)
        s_ref[0, hh] = gc_ref[hh] * s_old + _dot_tn(kd, v)
        if padded:
            o = o[0:c_data]
        mu = jnp.mean(o, axis=-1, keepdims=True)
        oc = o - mu
        var = jnp.mean(oc * oc, axis=-1, keepdims=True)
        o_n = oc * lax.rsqrt(var + GN_EPS) * gain_ref[hh]
        g = g_all[:, lo:hi].astype(F32)
        on_ref[:, lo:hi] = (o_n * (g * _sigmoid(g))).astype(on_ref.dtype)


def _retention(z, s0, tabs, gain, n_batch, n_chunks, c_data, out_dtype, n_heads):
    dec, qd, kd, gc = tabs
    m = z.shape[0]
    wblk = DK_RET * n_heads
    has_init = s0 is not None

    def zspec(col0):
        base = col0 // wblk
        return pl.BlockSpec((c_data, wblk), lambda b, hg, c: (b * n_chunks + c, base + hg))

    tab3 = lambda shape: pl.BlockSpec((n_heads,) + shape, lambda b, hg, c: (hg, 0, 0))
    sspec = pl.BlockSpec((1, n_heads, DK_RET, DV_RET), lambda b, hg, c: (b, hg, 0, 0))
    in_specs = [zspec(COL_QR), zspec(COL_KR), zspec(COL_VR), zspec(COL_GR)]
    args = [z, z, z, z]
    if has_init:
        in_specs.append(sspec)
        args.append(s0)
    in_specs += [tab3((RET_CHUNK, RET_CHUNK)), tab3((RET_CHUNK, 1)), tab3((RET_CHUNK, 1)),
                 tab3((1, 1)), tab3((1, DV_RET))]
    args += [dec, qd, kd, gc, gain]
    return pl.pallas_call(
        functools.partial(_retention_kernel, c_data=c_data, has_init=has_init, n_heads=n_heads),
        grid=(n_batch, H_RET // n_heads, n_chunks),
        in_specs=in_specs,
        out_specs=[
            pl.BlockSpec((c_data, wblk), lambda b, hg, c: (b * n_chunks + c, hg)),
            sspec,
        ],
        out_shape=[
            jax.ShapeDtypeStruct((m, RET_V), out_dtype),
            jax.ShapeDtypeStruct((n_batch, H_RET, DK_RET, DV_RET), F32),
        ],
        scratch_shapes=[pltpu.VMEM((3, RET_CHUNK, wblk), F32)],
        compiler_params=_params(("arbitrary", "arbitrary", "arbitrary")),
        name="retention",
    )(*args)


def _retention_tables(c_data):
    log_gamma = jnp.log1p(-jnp.exp2(-5.0 - jnp.arange(H_RET, dtype=F32)))
    idx = jnp.arange(RET_CHUNK, dtype=F32)
    live = idx < c_data
    diff = idx[:, None] - idx[None, :]
    causal = (diff >= 0) & live[:, None] & live[None, :]
    dec = jnp.where(causal, jnp.exp(log_gamma[:, None, None] * jnp.where(causal, diff, 0.0)), 0.0)
    qd = jnp.exp(log_gamma[:, None] * (idx[None, :] + 1.0))
    kd = jnp.where(live[None, :], jnp.exp(log_gamma[:, None] * (c_data - 1.0 - idx[None, :])), 0.0)
    gc = jnp.exp(log_gamma * c_data)
    return dec, qd[:, :, None], kd[:, :, None], gc[:, None, None]


def _moba_prompt_kernel(q_ref, k_ref, v_ref, o_ref, kb_ref, vb_ref, km_ref):
    nb = k_ref.shape[0] // MOBA_BLOCK
    blk = MOBA_BLOCK

    kb_ref[...] = k_ref[...].astype(BF16)
    vb_ref[...] = v_ref[...].astype(BF16)
    km_ref[...] = jnp.zeros(km_ref.shape, F32)
    for n in range(nb):
        km_ref[n:n + 1, :] = jnp.sum(k_ref[n * blk:(n + 1) * blk, :], axis=0, keepdims=True) * (1.0 / blk)
    kmb = km_ref[...].astype(BF16)

    lane = lax.broadcasted_iota(jnp.int32, (blk, 128), 1)
    row = lax.broadcasted_iota(jnp.int32, (blk, blk), 0)
    col = lax.broadcasted_iota(jnp.int32, (blk, blk), 1)
    causal_bias = jnp.where(col <= row, 0.0, NEG_BIG).astype(F32)

    for i in range(nb):
        q = q_ref[i * blk:(i + 1) * blk, :].astype(BF16)
        pieces = []
        if i > 0:
            s_el = jnp.where(lane < i, _dot_nt(q, kmb), -jnp.inf)
            cnt = jnp.zeros(s_el.shape, jnp.int32)
            for n in range(i):
                cn = s_el[:, n:n + 1]
                ahead = (cn > s_el) | ((cn == s_el) & (lane > n))
                cnt = cnt + ahead.astype(jnp.int32)
            bias = jnp.where((lane < i) & (cnt < MOBA_TOPK), 0.0, NEG_BIG).astype(F32)
            pieces = [jnp.broadcast_to(bias[:, n:n + 1], (blk, blk)) for n in range(i)]
        nk = (i + 1) * blk
        s = _dot_nt(q, kb_ref[0:nk, :]) + jnp.concatenate(pieces + [causal_bias], axis=1)
        m = jnp.max(s, axis=-1, keepdims=True)
        p = jnp.exp2((s - m) * SOFTMAX_C)
        l = jnp.sum(p, axis=-1, keepdims=True)
        o = _dot(p.astype(BF16), vb_ref[0:nk, :]) / l
        o_ref[i * blk:(i + 1) * blk, :] = o.astype(o_ref.dtype)


def _moba_prompt(z, k_new, v_new, n_batch, seq):
    m = z.shape[0]
    qcol = COL_QA // HD_ATT
    head_rows = lambda col0: pl.BlockSpec((seq, HD_ATT), lambda b, h: (b, col0 + h))
    return pl.pallas_call(
        _moba_prompt_kernel,
        grid=(n_batch, H_ATT),
        in_specs=[head_rows(qcol), head_rows(0), head_rows(0)],
        out_specs=head_rows(0),
        out_shape=jax.ShapeDtypeStruct((m, ATT_W), z.dtype),
        scratch_shapes=[
            pltpu.VMEM((seq, HD_ATT), BF16),
            pltpu.VMEM((seq, HD_ATT), BF16),
            pltpu.VMEM((128, HD_ATT), F32),
        ],
        compiler_params=_params(("arbitrary", "arbitrary")),
        name="moba_prompt",
    )(z, k_new, v_new)


SMP_G = 4
SMP_SUB = 512


def _moba_sample_kernel(pt_ref, q_ref, kn_ref, vn_ref, hb_ref, *rest, n_tok, n_past):
    n_pg = SMP_G * PAGES_PER_BLOCK
    ck = rest[:n_pg]
    cv = rest[n_pg:2 * n_pg]
    o_ref, m_all, l_all, s_all, o_sc, kpad, vpad = rest[2 * n_pg:]
    s_id = pl.program_id(1)
    rows = H_ATT * n_tok
    tok_shift = n_tok.bit_length() - 1
    assert n_tok == 1 << tok_shift and H_ATT & (H_ATT - 1) == 0 and rows <= 128

    @pl.when(s_id == 0)
    def _():
        m_all[...] = jnp.zeros(m_all.shape, F32)
        l_all[...] = jnp.zeros(l_all.shape, F32)
        s_all[...] = jnp.zeros(s_all.shape, F32)

    def by_head(ref):
        x = ref[...]
        return jnp.concatenate([x[:, h * HD_ATT:(h + 1) * HD_ATT] for h in range(H_ATT)], axis=0)

    qb = by_head(q_ref).astype(BF16)
    qr = qb.astype(F32)
    hb = hb_ref[...]
    lane = lax.broadcasted_iota(jnp.int32, (rows, 128), 1)
    subs = [slice(t * SMP_SUB, (t + 1) * SMP_SUB) for t in range(PAGE_SIZE * H_ATT // SMP_SUB)]
    ppb = PAGES_PER_BLOCK * len(subs)
    ppb_shift = ppb.bit_length() - 1
    n_parts = n_past * ppb
    assert ppb == 1 << ppb_shift and n_parts <= 128

    pieces = []
    for g in range(SMP_G):
        for pg in range(PAGES_PER_BLOCK):
            for t, rs in enumerate(subs):
                pieces.append((g * ppb + pg * len(subs) + t, g, PAGES_PER_BLOCK * g + pg, rs))
    first = s_id * (SMP_G * ppb)

    ksum = [jnp.zeros((H_ATT, HD_ATT), F32) for _ in range(SMP_G)]
    scores = []
    for _, g, pi, rs in pieces:
        kt = ck[pi][rs, :]
        ksum[g] = ksum[g] + jnp.sum(kt.reshape(SMP_SUB // H_ATT, H_ATT, HD_ATT), axis=0)
        scores.append(_dot_nt(qb, kt.astype(BF16)))
    probs, stats = [], []
    for s in scores:
        s = s + hb
        m_t = jnp.max(s, axis=-1, keepdims=True)
        p = jnp.exp2((s - m_t) * SOFTMAX_C)
        probs.append(p.astype(BF16))
        stats.append((m_t, jnp.sum(p, axis=-1, keepdims=True)))
    for (local, _, pi, rs), p in zip(pieces, probs):
        o_sc[first + local] = _dot(p, cv[pi][rs, :].astype(BF16))

    m_new, l_new, s_new = m_all[...], l_all[...], s_all[...]
    for g in range(SMP_G):
        kmean = (ksum[g] * (1.0 / MOBA_BLOCK)).astype(BF16).astype(F32)
        kexp = jnp.concatenate(
            [jnp.broadcast_to(kmean[h:h + 1, :], (n_tok, HD_ATT)) for h in range(H_ATT)], axis=0)
        sc = jnp.sum(qr * kexp, axis=-1, keepdims=True)
        for (local, gg, _, _), (m_t, l_t) in zip(pieces, stats):
            if gg == g:
                here = lane == first + local
                m_new = jnp.where(here, m_t, m_new)
                l_new = jnp.where(here, l_t, l_new)
                s_new = jnp.where(here, sc, s_new)
    m_all[...] = m_new
    l_all[...] = l_new
    s_all[...] = s_new

    @pl.when(s_id == pl.num_programs(1) - 1)
    def _():
        past = lane < n_parts
        lane_blk = lane >> ppb_shift
        sc_all = jnp.where(past, s_all[...], -jnp.inf)
        cnt = jnp.zeros(sc_all.shape, jnp.int32)
        for nn in range(n_past):
            col = sc_all[:, nn * ppb:nn * ppb + 1]
            ahead = (col > sc_all) | ((col == sc_all) & (lane_blk > nn))
            cnt = cnt + ahead.astype(jnp.int32)
        sel = past & (cnt < MOBA_TOPK)
        m_sel = jnp.where(sel, m_all[...], NEG_BIG)

        kpad[...] = jnp.zeros(kpad.shape, F32)
        vpad[...] = jnp.zeros(vpad.shape, F32)
        kpad[0:rows, :] = by_head(kn_ref)
        vpad[0:rows, :] = by_head(vn_ref)
        r_id = lax.broadcasted_iota(jnp.int32, (rows, 128), 0)
        own_ok = ((lane < rows) & ((lane >> tok_shift) == (r_id >> tok_shift))
                  & ((lane & (n_tok - 1)) <= (r_id & (n_tok - 1))))
        s_own = jnp.where(own_ok, _dot_nt(qb, kpad[...].astype(BF16)), NEG_BIG)
        big = jnp.maximum(jnp.max(m_sel, axis=-1, keepdims=True), jnp.max(s_own, axis=-1, keepdims=True))
        w = jnp.where(sel, jnp.exp2((m_sel - big) * SOFTMAX_C), 0.0)
        p_own = jnp.exp2((s_own - big) * SOFTMAX_C)
        den = jnp.sum(w * l_all[...], axis=-1, keepdims=True) + jnp.sum(p_own, axis=-1, keepdims=True)
        num = _dot(p_own.astype(BF16), vpad[...].astype(BF16))
        for pp in range(n_parts):
            num = num + w[:, pp:pp + 1] * o_sc[pp]
        out = num / den
        for h in range(H_ATT):
            o_ref[:, h * HD_ATT:(h + 1) * HD_ATT] = out[h * n_tok:(h + 1) * n_tok, :].astype(o_ref.dtype)


def _moba_sample(z, k_new, v_new, cache_k, cache_v, page_table, n_batch, n_tok):
    n_pages = page_table.shape[1]
    n_past = n_pages // PAGES_PER_BLOCK
    n_pg = SMP_G * PAGES_PER_BLOCK
    n_steps = n_past // SMP_G
    pt = page_table.reshape(-1)
    ck = cache_k.reshape(cache_k.shape[0], PAGE_SIZE * H_ATT, HD_ATT)
    cv = cache_v.reshape(cache_v.shape[0], PAGE_SIZE * H_ATT, HD_ATT)

    def page_spec(jj):
        return pl.BlockSpec((None, PAGE_SIZE * H_ATT, HD_ATT),
                            lambda b, s, pt_ref: (pt_ref[b * n_pages + s * n_pg + jj], 0, 0))

    row = lambda col: pl.BlockSpec((n_tok, ATT_W), lambda b, s, pt_ref: (b, col))
    rows = H_ATT * n_tok
    pg_rows = PAGE_SIZE * H_ATT
    row_head = jnp.arange(rows, dtype=jnp.int32)[:, None] // n_tok
    col_head = jnp.arange(SMP_SUB, dtype=jnp.int32)[None, :] % H_ATT
    head_bias = jnp.where(row_head == col_head, 0.0, NEG_BIG).astype(F32)
    n_parts = n_past * pg_rows // SMP_SUB * PAGES_PER_BLOCK
    grid_spec = pltpu.PrefetchScalarGridSpec(
        num_scalar_prefetch=1,
        grid=(n_batch, n_steps),
        in_specs=[row(COL_QA // ATT_W), row(0), row(0),
                  pl.BlockSpec((rows, SMP_SUB), lambda b, s, pt_ref: (0, 0))]
        + [page_spec(jj) for jj in range(n_pg)] + [page_spec(jj) for jj in range(n_pg)],
        out_specs=row(0),
        scratch_shapes=[
            pltpu.VMEM((rows, 128), F32),
            pltpu.VMEM((rows, 128), F32),
            pltpu.VMEM((rows, 128), F32),
            pltpu.VMEM((n_parts, rows, HD_ATT), F32),
            pltpu.VMEM((128, HD_ATT), F32),
            pltpu.VMEM((128, HD_ATT), F32),
        ],
    )
    return pl.pallas_call(
        functools.partial(_moba_sample_kernel, n_tok=n_tok, n_past=n_past),
        grid_spec=grid_spec,
        out_shape=jax.ShapeDtypeStruct((z.shape[0], ATT_W), z.dtype),
        compiler_params=_params(("arbitrary", "arbitrary")),
        name="moba_sample",
    )(pt, z, k_new, v_new, head_bias, *([ck] * n_pg), *([cv] * n_pg))


MIX_TN = 1024
DOT_TN = 512


def _gate_mix_kernel(on_ref, oa_ref, wr_ref, wa_ref, gbr_ref, gba_ref, o_ref):
    on = on_ref[...].astype(BF16)
    oa = oa_ref[...].astype(BF16)
    for t in range(MIX_TN // DOT_TN):
        sl = slice(t * DOT_TN, (t + 1) * DOT_TN)
        ret = _dot(on, wr_ref[:, sl])
        att = _dot(oa, wa_ref[:, sl])
        mix = _sigmoid(gbr_ref[:, sl].astype(F32)) * ret + _sigmoid(gba_ref[:, sl].astype(F32)) * att
        o_ref[:, sl] = mix.astype(o_ref.dtype)


def _gate_mix(on, oa, z, w_ret, w_att, tm):
    m = on.shape[0]
    jr = COL_GBR // MIX_TN
    ja = COL_GBA // MIX_TN
    return pl.pallas_call(
        _gate_mix_kernel,
        grid=(m // tm, D_MODEL // MIX_TN),
        in_specs=[
            pl.BlockSpec((tm, RET_V), lambda i, j: (i, 0)),
            pl.BlockSpec((tm, ATT_W), lambda i, j: (i, 0)),
            pl.BlockSpec((RET_V, MIX_TN), lambda i, j: (0, j)),
            pl.BlockSpec((ATT_W, MIX_TN), lambda i, j: (0, j)),
            pl.BlockSpec((tm, MIX_TN), lambda i, j: (i, jr + j)),
            pl.BlockSpec((tm, MIX_TN), lambda i, j: (i, ja + j)),
        ],
        out_specs=pl.BlockSpec((tm, MIX_TN), lambda i, j: (i, j)),
        out_shape=jax.ShapeDtypeStruct((m, D_MODEL), BF16),
        compiler_params=_params(("arbitrary", "arbitrary")),
        name="gate_mix",
    )(on, oa, w_ret, w_att, z, z)


OUT_ROWS = 256


def _out_ln_kernel(mix_ref, w_ref, x_ref, g_ref, b_ref, h_ref, hb_ref):
    for r in range(mix_ref.shape[0] // OUT_ROWS):
        rs = slice(r * OUT_ROWS, (r + 1) * OUT_ROWS)
        mix = _dot(mix_ref[rs, :], w_ref[...])
        h = _layer_norm_rows(ALPHA * x_ref[rs, :] + mix, g_ref[...], b_ref[...])
        h_ref[rs, :] = h
        hb_ref[rs, :] = h.astype(BF16)


def _out_ln(mix_in, w_out, x, g, b, tm):
    m = x.shape[0]
    rowspec = pl.BlockSpec((tm, D_MODEL), lambda i: (i, 0))
    vec = pl.BlockSpec((1, D_MODEL), lambda i: (0, 0))
    return pl.pallas_call(
        _out_ln_kernel,
        grid=(m // tm,),
        in_specs=[rowspec, pl.BlockSpec((D_MODEL, D_MODEL), lambda i: (0, 0)), rowspec, vec, vec],
        out_specs=[rowspec, rowspec],
        out_shape=[jax.ShapeDtypeStruct((m, D_MODEL), F32), jax.ShapeDtypeStruct((m, D_MODEL), BF16)],
        compiler_params=_params(("arbitrary",)),
        name="out_ln",
    )(mix_in, w_out, x, g, b)


MLP_TF = 1024


def _mlp_ln_kernel(h_ref, hb_ref, wu_ref, wd_ref, g_ref, b_ref, y_ref, acc_ref):
    f = pl.program_id(1)

    @pl.when(f == 0)
    def _():
        acc_ref[...] = ALPHA * h_ref[...]

    hb = hb_ref[...]
    pieces = []
    for c in range(MLP_TF // DOT_TN):
        a = jnp.maximum(_dot(hb, wu_ref[:, c * DOT_TN:(c + 1) * DOT_TN]), 0.0)
        pieces.append((a * a).astype(BF16))
    act = jnp.concatenate(pieces, axis=1)
    for n in range(D_MODEL // DOT_TN):
        sl = slice(n * DOT_TN, (n + 1) * DOT_TN)
        acc_ref[:, sl] += _dot(act, wd_ref[:, sl])

    @pl.when(f == pl.num_programs(1) - 1)
    def _():
        y_ref[...] = _layer_norm_rows(acc_ref[...], g_ref[...], b_ref[...])


def _mlp_ln(h, hb, w_up, w_down, g, b, tm):
    m = h.shape[0]
    rowspec = pl.BlockSpec((tm, D_MODEL), lambda i, f: (i, 0))
    vec = pl.BlockSpec((1, D_MODEL), lambda i, f: (0, 0))
    return pl.pallas_call(
        _mlp_ln_kernel,
        grid=(m // tm, D_FF // MLP_TF),
        in_specs=[
            rowspec, rowspec,
            pl.BlockSpec((D_MODEL, MLP_TF), lambda i, f: (0, f)),
            pl.BlockSpec((MLP_TF, D_MODEL), lambda i, f: (f, 0)),
            vec, vec,
        ],
        out_specs=rowspec,
        out_shape=jax.ShapeDtypeStruct((m, D_MODEL), F32),
        scratch_shapes=[pltpu.VMEM((tm, D_MODEL), F32)],
        compiler_params=_params(("arbitrary", "arbitrary")),
        name="mlp_ln",
    )(h, hb, w_up, w_down, g, b)


def _rotary_tables(pos):
    half = DK_RET // 2
    inv = ROPE_BASE ** (-jnp.arange(half, dtype=F32) / half)
    ang = pos.astype(F32)[:, None] * inv[None, :]
    return jnp.cos(ang), jnp.sin(ang)


def _merge(x, z, on, oa, w, tm):
    mix_in = _gate_mix(on, oa, z, w["ret"], w["att"], tm)
    h, hb = _out_ln(mix_in, w["out"], x, w["ln1_g"], w["ln1_b"], tm)
    return _mlp_ln(h, hb, w["up"], w["down"], w["ln2_g"], w["ln2_b"], tm)


def kernel(x_prompt, x_sample, cache_k, cache_v, state_ret, page_table, w_in, ret_gn_gain, w_ret_br,
           w_att_br, w_out, ln1_g, ln1_b, w_up, w_down, ln2_g, ln2_b):
    n_b, seq, _ = x_prompt.shape
    n_db, n_tok, _ = x_sample.shape
    assert w_in.shape[0] == DEPTH == 1
    l = 0
    w = {
        "in": w_in[l].astype(BF16), "ret": w_ret_br[l].astype(BF16), "att": w_att_br[l].astype(BF16),
        "out": w_out[l].astype(BF16), "up": w_up[l].astype(BF16), "down": w_down[l].astype(BF16),
        "ln1_g": ln1_g[l][None, :], "ln1_b": ln1_b[l][None, :],
        "ln2_g": ln2_g[l][None, :], "ln2_b": ln2_b[l][None, :],
    }
    gain = ret_gn_gain[l][:, None, :]

    xp = x_prompt.reshape(n_b * seq, D_MODEL)
    cos_p, sin_p = _rotary_tables(jnp.arange(seq, dtype=jnp.int32))
    tm_p = 1024
    z_p, k_p, v_p = _in_proj(xp, w["in"], cos_p, sin_p, tm_p, BF16)
    on_p, s_p = _retention(z_p, None, _retention_tables(RET_CHUNK), gain, n_b, seq // RET_CHUNK,
                           RET_CHUNK, BF16, n_heads=4)
    oa_p = _moba_prompt(z_p, k_p, v_p, n_b, seq)
    y_p = _merge(xp, z_p, on_p, oa_p, w, 512)

    xs = x_sample.reshape(n_db * n_tok, D_MODEL)
    pos_s = PAST_LEN + jnp.arange(n_tok, dtype=jnp.int32)
    cos_s, sin_s = _rotary_tables(jnp.tile(pos_s, n_db))
    tm_s = n_db * n_tok
    z_s, k_s, v_s = _in_proj(xs, w["in"], cos_s, sin_s, tm_s, F32)
    on_s, s_s = _retention(z_s, state_ret[l], _retention_tables(n_tok), gain, n_db, 1, n_tok, F32,
                           n_heads=H_RET)
    oa_s = _moba_sample(z_s, k_s, v_s, cache_k[l], cache_v[l], page_table, n_db, n_tok)
    y_s = _merge(xs, z_s, on_s, oa_s, w, tm_s)

    return (
        y_p.reshape(n_b, seq, D_MODEL),
        y_s.reshape(n_db, n_tok, D_MODEL),
        k_p.reshape(1, n_b, seq, H_ATT, HD_ATT),
        v_p.reshape(1, n_b, seq, H_ATT, HD_ATT),
        s_p[None],
        k_s.reshape(1, n_db, n_tok, H_ATT, HD_ATT),
        v_s.reshape(1, n_db, n_tok, H_ATT, HD_ATT),
        s_s[None],
    )
```

```python
import functools
import math

import jax
import jax.numpy as jnp
from jax import lax
from jax.experimental import pallas as pl
from jax.experimental.pallas import tpu as pltpu

F32 = jnp.float32
BF16 = jnp.bfloat16

D_MODEL = 2048
DEPTH = 1
PAST_LEN = 8192
PAGE_SIZE = 128
H_RET = 8
DK_RET = 256
DV_RET = 256
RET_CHUNK = 128
H_ATT = 8
HD_ATT = 128
MOBA_BLOCK = 256
MOBA_TOPK = 3
D_FF = 4 * D_MODEL
RET_QK = H_RET * DK_RET
RET_V = H_RET * DV_RET
ATT_W = H_ATT * HD_ATT
W_IN_COLS = 2 * RET_QK + 2 * RET_V + 3 * ATT_W + 2 * D_MODEL
ROPE_BASE = 10000.0
LN_EPS = 1e-5
GN_EPS = 1e-5
ALPHA = (2 * DEPTH) ** 0.25

COL_QR = 0
COL_KR = RET_QK
COL_VR = 2 * RET_QK
COL_GR = 2 * RET_QK + RET_V
COL_QA = 2 * RET_QK + 2 * RET_V
COL_KA = COL_QA + ATT_W
COL_VA = COL_KA + ATT_W
COL_GBR = COL_VA + ATT_W
COL_GBA = COL_GBR + D_MODEL

NEG_BIG = -1e30
SOFTMAX_C = HD_ATT ** -0.5 * math.log2(math.e)
PAGES_PER_BLOCK = MOBA_BLOCK // PAGE_SIZE
VMEM_LIMIT = 56 * 1024 * 1024


def _params(semantics):
    return pltpu.CompilerParams(dimension_semantics=semantics, vmem_limit_bytes=VMEM_LIMIT)


def _sigmoid(x):
    return 1.0 / (1.0 + jnp.exp(-x))


def _dot(a, b):
    return jnp.dot(a, b, preferred_element_type=F32)


def _dot_nt(a, b):
    return lax.dot_general(a, b, (((1,), (1,)), ((), ())), preferred_element_type=F32)


def _dot_tn(a, b):
    return lax.dot_general(a, b, (((0,), (0,)), ((), ())), preferred_element_type=F32)


def _layer_norm_rows(x, g, b):
    mu = jnp.mean(x, axis=-1, keepdims=True)
    xc = x - mu
    var = jnp.mean(xc * xc, axis=-1, keepdims=True)
    return xc * lax.rsqrt(var + LN_EPS) * g + b


IN_TN = 512
IN_STEP = 1024
_J_ROT_END = COL_VR // IN_STEP
_J_KR = COL_KR // IN_STEP
_J_KA = COL_KA // IN_STEP
_J_VA = COL_VA // IN_STEP
assert ATT_W == IN_STEP and COL_VR % IN_STEP == 0 and COL_QA % IN_STEP == 0


def _in_proj_kernel(*refs):
    _in_proj_body(pl.program_id(1), None, *refs)


def _in_proj_body(j, active, x_ref, w_ref, cos_ref, sin_ref, z_ref, k_ref, v_ref, xb_ref):
    tiles = [slice(t * IN_TN, (t + 1) * IN_TN) for t in range(IN_STEP // IN_TN)]
    when = lambda cond: pl.when(cond if active is None else active & cond)

    @when(j == 0)
    def _():
        xb_ref[...] = x_ref[...].astype(BF16)

    def tile_dot(sl):
        return _dot(xb_ref[...], w_ref[:, sl])

    @when(j < _J_ROT_END)
    def _():
        cos = cos_ref[...]
        sin = sin_ref[...]
        scale = jnp.where(j >= _J_KR, DK_RET ** -0.5, 1.0).astype(F32)
        half = DK_RET // 2
        for sl in tiles:
            acc = tile_dot(sl)
            for hh in range(IN_TN // DK_RET):
                c0 = hh * DK_RET
                o0 = sl.start + c0
                x1 = acc[:, c0:c0 + half]
                x2 = acc[:, c0 + half:c0 + DK_RET]
                z_ref[:, o0:o0 + half] = ((x1 * cos - x2 * sin) * scale).astype(z_ref.dtype)
                z_ref[:, o0 + half:o0 + DK_RET] = ((x1 * sin + x2 * cos) * scale).astype(z_ref.dtype)

    @when((j >= _J_ROT_END) & (j != _J_KA) & (j != _J_VA))
    def _():
        for sl in tiles:
            z_ref[:, sl] = tile_dot(sl).astype(z_ref.dtype)

    def new_rows(out_ref):
        for sl in tiles:
            acc = tile_dot(sl)
            z_ref[:, sl] = acc.astype(z_ref.dtype)
            out_ref[:, sl] = acc

    when(j == _J_KA)(lambda: new_rows(k_ref))
    when(j == _J_VA)(lambda: new_rows(v_ref))


def _in_proj(x, w_bf, cos_t, sin_t, tm, z_dtype):
    m = x.shape[0]
    n_tab = cos_t.shape[0] // tm
    return pl.pallas_call(
        _in_proj_kernel,
        grid=(m // tm, W_IN_COLS // IN_STEP),
        in_specs=[
            pl.BlockSpec((tm, D_MODEL), lambda i, j: (i, 0), pipeline_mode=pl.Buffered(1)),
            pl.BlockSpec((D_MODEL, IN_STEP), lambda i, j: (0, j)),
            pl.BlockSpec((tm, DK_RET // 2), lambda i, j: (i % n_tab, 0)),
            pl.BlockSpec((tm, DK_RET // 2), lambda i, j: (i % n_tab, 0)),
        ],
        out_specs=[
            pl.BlockSpec((tm, IN_STEP), lambda i, j: (i, j)),
            pl.BlockSpec((tm, ATT_W), lambda i, j: (i, 0)),
            pl.BlockSpec((tm, ATT_W), lambda i, j: (i, 0)),
        ],
        out_shape=[
            jax.ShapeDtypeStruct((m, W_IN_COLS), z_dtype),
            jax.ShapeDtypeStruct((m, ATT_W), F32),
            jax.ShapeDtypeStruct((m, ATT_W), F32),
        ],
        scratch_shapes=[pltpu.VMEM((tm, D_MODEL), BF16)],
        compiler_params=_params(("arbitrary", "arbitrary")),
        name="in_proj",
    )(x, w_bf, cos_t, sin_t)


def _retention_kernel(*refs, c_data, has_init, n_heads):
    if has_init:
        (q_ref, k_ref, v_ref, g_ref, s0_ref, dec_ref, qd_ref, kd_ref, gc_ref, gain_ref,
         on_ref, s_ref, pad_ref) = refs
    else:
        (q_ref, k_ref, v_ref, g_ref, dec_ref, qd_ref, kd_ref, gc_ref, gain_ref,
         on_ref, s_ref, pad_ref) = refs
        s0_ref = None
    c = pl.program_id(2)
    padded = c_data != RET_CHUNK

    @pl.when(c == 0)
    def _():
        if has_init:
            s_ref[...] = s0_ref[...]
        else:
            s_ref[...] = jnp.zeros(s_ref.shape, F32)

    def load(ref, slot):
        if not padded:
            return ref[...]
        pad_ref[slot] = jnp.zeros(pad_ref.shape[1:], F32)
        pad_ref[slot, 0:c_data, :] = ref[...].astype(F32)
        return pad_ref[slot]

    q_all = load(q_ref, 0)
    k_all = load(k_ref, 1)
    v_all = load(v_ref, 2)
    g_all = g_ref[...]

    for hh in range(n_heads):
        lo, hi = hh * DK_RET, (hh + 1) * DK_RET
        q = q_all[:, lo:hi].astype(BF16)
        k = k_all[:, lo:hi]
        v = v_all[:, lo:hi].astype(BF16)
        s_old = s_ref[0, hh]
        scores = _dot_nt(q, k.astype(BF16)) * dec_ref[hh]
        inner = _dot(scores.astype(BF16), v)
        cross = _dot(q, s_old.astype(BF16)) * qd_ref[hh]
        o = inner + cross
        kd = (k.astype(F32) * kd_ref[hh]).astype(BF16)
        s_ref[0, hh] = gc_ref[hh] * s_old + _dot_tn(kd, v)
        if padded:
            o = o[0:c_data]
        mu = jnp.mean(o, axis=-1, keepdims=True)
        oc = o - mu
        var = jnp.mean(oc * oc, axis=-1, keepdims=True)
        o_n = oc * lax.rsqrt(var + GN_EPS) * gain_ref[hh]
        g = g_all[:, lo:hi].astype(F32)
        on_ref[:, lo:hi] = (o_n * (g * _sigmoid(g))).astype(on_ref.dtype)


def _retention(z, s0, tabs, gain, n_batch, n_chunks, c_data, out_dtype, n_heads):
    dec, qd, kd, gc = tabs
    m = z.shape[0]
    wblk = DK_RET * n_heads
    has_init = s0 is not None

    def zspec(col0):
        base = col0 // wblk
        return pl.BlockSpec((c_data, wblk), lambda b, hg, c: (b * n_chunks + c, base + hg))

    tab3 = lambda shape: pl.BlockSpec((n_heads,) + shape, lambda b, hg, c: (hg, 0, 0))
    sspec = pl.BlockSpec((1, n_heads, DK_RET, DV_RET), lambda b, hg, c: (b, hg, 0, 0))
    in_specs = [zspec(COL_QR), zspec(COL_KR), zspec(COL_VR), zspec(COL_GR)]
    args = [z, z, z, z]
    if has_init:
        in_specs.append(sspec)
        args.append(s0)
    in_specs += [tab3((RET_CHUNK, RET_CHUNK)), tab3((RET_CHUNK, 1)), tab3((RET_CHUNK, 1)),
                 tab3((1, 1)), tab3((1, DV_RET))]
    args += [dec, qd, kd, gc, gain]
    return pl.pallas_call(
        functools.partial(_retention_kernel, c_data=c_data, has_init=has_init, n_heads=n_heads),
        grid=(n_batch, H_RET // n_heads, n_chunks),
        in_specs=in_specs,
        out_specs=[
            pl.BlockSpec((c_data, wblk), lambda b, hg, c: (b * n_chunks + c, hg)),
            sspec,
        ],
        out_shape=[
            jax.ShapeDtypeStruct((m, RET_V), out_dtype),
            jax.ShapeDtypeStruct((n_batch, H_RET, DK_RET, DV_RET), F32),
        ],
        scratch_shapes=[pltpu.VMEM((3, RET_CHUNK, wblk), F32)],
        compiler_params=_params(("arbitrary", "arbitrary", "arbitrary")),
        name="retention",
    )(*args)


def _retention_tables(c_data):
    log_gamma = jnp.log1p(-jnp.exp2(-5.0 - jnp.arange(H_RET, dtype=F32)))
    idx = jnp.arange(RET_CHUNK, dtype=F32)
    live = idx < c_data
    diff = idx[:, None] - idx[None, :]
    causal = (diff >= 0) & live[:, None] & live[None, :]
    dec = jnp.where(causal, jnp.exp(log_gamma[:, None, None] * jnp.where(causal, diff, 0.0)), 0.0)
    qd = jnp.exp(log_gamma[:, None] * (idx[None, :] + 1.0))
    kd = jnp.where(live[None, :], jnp.exp(log_gamma[:, None] * (c_data - 1.0 - idx[None, :])), 0.0)
    gc = jnp.exp(log_gamma * c_data)
    return dec, qd[:, :, None], kd[:, :, None], gc[:, None, None]


def _moba_prompt_kernel(q_ref, k_ref, v_ref, o_ref, kb_ref, vb_ref, km_ref):
    nb = k_ref.shape[0] // MOBA_BLOCK
    blk = MOBA_BLOCK

    kb_ref[...] = k_ref[...].astype(BF16)
    vb_ref[...] = v_ref[...].astype(BF16)
    km_ref[...] = jnp.zeros(km_ref.shape, F32)
    for n in range(nb):
        km_ref[n:n + 1, :] = jnp.sum(k_ref[n * blk:(n + 1) * blk, :], axis=0, keepdims=True) * (1.0 / blk)
    kmb = km_ref[...].astype(BF16)

    lane = lax.broadcasted_iota(jnp.int32, (blk, 128), 1)
    row = lax.broadcasted_iota(jnp.int32, (blk, blk), 0)
    col = lax.broadcasted_iota(jnp.int32, (blk, blk), 1)
    causal_bias = jnp.where(col <= row, 0.0, NEG_BIG).astype(F32)

    for i in range(nb):
        q = q_ref[i * blk:(i + 1) * blk, :].astype(BF16)
        pieces = []
        if i > 0:
            s_el = jnp.where(lane < i, _dot_nt(q, kmb), -jnp.inf)
            cnt = jnp.zeros(s_el.shape, jnp.int32)
            for n in range(i):
                cn = s_el[:, n:n + 1]
                ahead = (cn > s_el) | ((cn == s_el) & (lane > n))
                cnt = cnt + ahead.astype(jnp.int32)
            bias = jnp.where((lane < i) & (cnt < MOBA_TOPK), 0.0, NEG_BIG).astype(F32)
            pieces = [jnp.broadcast_to(bias[:, n:n + 1], (blk, blk)) for n in range(i)]
        nk = (i + 1) * blk
        s = _dot_nt(q, kb_ref[0:nk, :]) + jnp.concatenate(pieces + [causal_bias], axis=1)
        m = jnp.max(s, axis=-1, keepdims=True)
        p = jnp.exp2((s - m) * SOFTMAX_C)
        l = jnp.sum(p, axis=-1, keepdims=True)
        o = _dot(p.astype(BF16), vb_ref[0:nk, :]) / l
        o_ref[i * blk:(i + 1) * blk, :] = o.astype(o_ref.dtype)


def _moba_prompt(z, k_new, v_new, n_batch, seq):
    m = z.shape[0]
    qcol = COL_QA // HD_ATT
    head_rows = lambda col0: pl.BlockSpec((seq, HD_ATT), lambda b, h: (b, col0 + h))
    return pl.pallas_call(
        _moba_prompt_kernel,
        grid=(n_batch, H_ATT),
        in_specs=[head_rows(qcol), head_rows(0), head_rows(0)],
        out_specs=head_rows(0),
        out_shape=jax.ShapeDtypeStruct((m, ATT_W), z.dtype),
        scratch_shapes=[
            pltpu.VMEM((seq, HD_ATT), BF16),
            pltpu.VMEM((seq, HD_ATT), BF16),
            pltpu.VMEM((128, HD_ATT), F32),
        ],
        compiler_params=_params(("arbitrary", "arbitrary")),
        name="moba_prompt",
    )(z, k_new, v_new)


SMP_G = 4
SMP_SUB = 512


def _moba_sample_body(s_id, n_steps, q_ref, kn_ref, vn_ref, hb_ref, ck, cv,
                      o_ref, m_all, l_all, s_all, o_sc, kpad, vpad, *, n_tok, n_past):
    rows = H_ATT * n_tok
    tok_shift = n_tok.bit_length() - 1
    assert n_tok == 1 << tok_shift and H_ATT & (H_ATT - 1) == 0 and rows <= 128

    @pl.when(s_id == 0)
    def _():
        m_all[...] = jnp.zeros(m_all.shape, F32)
        l_all[...] = jnp.zeros(l_all.shape, F32)
        s_all[...] = jnp.zeros(s_all.shape, F32)

    def by_head(ref):
        x = ref[...]
        return jnp.concatenate([x[:, h * HD_ATT:(h + 1) * HD_ATT] for h in range(H_ATT)], axis=0)

    qb = by_head(q_ref).astype(BF16)
    qr = qb.astype(F32)
    hb = hb_ref[...]
    lane = lax.broadcasted_iota(jnp.int32, (rows, 128), 1)
    subs = [slice(t * SMP_SUB, (t + 1) * SMP_SUB) for t in range(PAGE_SIZE * H_ATT // SMP_SUB)]
    ppb = PAGES_PER_BLOCK * len(subs)
    ppb_shift = ppb.bit_length() - 1
    n_parts = n_past * ppb
    assert ppb == 1 << ppb_shift and n_parts <= 128

    pieces = []
    for g in range(SMP_G):
        for pg in range(PAGES_PER_BLOCK):
            for t, rs in enumerate(subs):
                pieces.append((g * ppb + pg * len(subs) + t, g, PAGES_PER_BLOCK * g + pg, rs))
    first = s_id * (SMP_G * ppb)

    ksum = [jnp.zeros((H_ATT, HD_ATT), F32) for _ in range(SMP_G)]
    scores = []
    for _, g, pi, rs in pieces:
        kt = ck[pi][rs, :]
        ksum[g] = ksum[g] + jnp.sum(kt.reshape(SMP_SUB // H_ATT, H_ATT, HD_ATT), axis=0)
        scores.append(_dot_nt(qb, kt.astype(BF16)))
    probs, stats = [], []
    for s in scores:
        s = s + hb
        m_t = jnp.max(s, axis=-1, keepdims=True)
        p = jnp.exp2((s - m_t) * SOFTMAX_C)
        probs.append(p.astype(BF16))
        stats.append((m_t, jnp.sum(p, axis=-1, keepdims=True)))
    for (local, _, pi, rs), p in zip(pieces, probs):
        o_sc[first + local] = _dot(p, cv[pi][rs, :].astype(BF16))

    m_new, l_new, s_new = m_all[...], l_all[...], s_all[...]
    for g in range(SMP_G):
        kmean = (ksum[g] * (1.0 / MOBA_BLOCK)).astype(BF16).astype(F32)
        kexp = jnp.concatenate(
            [jnp.broadcast_to(kmean[h:h + 1, :], (n_tok, HD_ATT)) for h in range(H_ATT)], axis=0)
        sc = jnp.sum(qr * kexp, axis=-1, keepdims=True)
        for (local, gg, _, _), (m_t, l_t) in zip(pieces, stats):
            if gg == g:
                here = lane == first + local
                m_new = jnp.where(here, m_t, m_new)
                l_new = jnp.where(here, l_t, l_new)
                s_new = jnp.where(here, sc, s_new)
    m_all[...] = m_new
    l_all[...] = l_new
    s_all[...] = s_new

    @pl.when(s_id == n_steps - 1)
    def _():
        past = lane < n_parts
        lane_blk = lane >> ppb_shift
        sc_all = jnp.where(past, s_all[...], -jnp.inf)
        cnt = jnp.zeros(sc_all.shape, jnp.int32)
        for nn in range(n_past):
            col = sc_all[:, nn * ppb:nn * ppb + 1]
            ahead = (col > sc_all) | ((col == sc_all) & (lane_blk > nn))
            cnt = cnt + ahead.astype(jnp.int32)
        sel = past & (cnt < MOBA_TOPK)
        m_sel = jnp.where(sel, m_all[...], NEG_BIG)

        kpad[...] = jnp.zeros(kpad.shape, F32)
        vpad[...] = jnp.zeros(vpad.shape, F32)
        kpad[0:rows, :] = by_head(kn_ref)
        vpad[0:rows, :] = by_head(vn_ref)
        r_id = lax.broadcasted_iota(jnp.int32, (rows, 128), 0)
        own_ok = ((lane < rows) & ((lane >> tok_shift) == (r_id >> tok_shift))
                  & ((lane & (n_tok - 1)) <= (r_id & (n_tok - 1))))
        s_own = jnp.where(own_ok, _dot_nt(qb, kpad[...].astype(BF16)), NEG_BIG)
        big = jnp.maximum(jnp.max(m_sel, axis=-1, keepdims=True), jnp.max(s_own, axis=-1, keepdims=True))
        w = jnp.where(sel, jnp.exp2((m_sel - big) * SOFTMAX_C), 0.0)
        p_own = jnp.exp2((s_own - big) * SOFTMAX_C)
        den = jnp.sum(w * l_all[...], axis=-1, keepdims=True) + jnp.sum(p_own, axis=-1, keepdims=True)
        num = _dot(p_own.astype(BF16), vpad[...].astype(BF16))
        for pp in range(n_parts):
            num = num + w[:, pp:pp + 1] * o_sc[pp]
        out = num / den
        for h in range(H_ATT):
            o_ref[:, h * HD_ATT:(h + 1) * HD_ATT] = out[h * n_tok:(h + 1) * n_tok, :].astype(o_ref.dtype)


def _proj_moba_kernel(pt_ref, x_ref, w_ref, cos_ref, sin_ref, q_ref, kn_ref, vn_ref, hb_ref, *rest,
                      n_proj, n_col, smp_steps, n_tok, n_past):
    n_pg = SMP_G * PAGES_PER_BLOCK
    ck = rest[:n_pg]
    cv = rest[n_pg:2 * n_pg]
    z_ref, k_ref, v_ref, o_ref, xb_ref, m_all, l_all, s_all, o_sc, kpad, vpad = rest[2 * n_pg:]
    t = pl.program_id(0)
    _in_proj_body(jnp.minimum(t, n_proj - 1) % n_col, t < n_proj,
                  x_ref, w_ref, cos_ref, sin_ref, z_ref, k_ref, v_ref, xb_ref)
    _moba_sample_body(t % smp_steps, smp_steps, q_ref, kn_ref, vn_ref, hb_ref, ck, cv,
                      o_ref, m_all, l_all, s_all, o_sc, kpad, vpad, n_tok=n_tok, n_past=n_past)


def _in_proj_with_moba_sample(x, w_bf, cos_t, sin_t, tm, z_s, k_s, v_s, cache_k, cache_v, page_table,
                              n_batch, n_tok):
    m = x.shape[0]
    n_col = W_IN_COLS // IN_STEP
    n_proj = (m // tm) * n_col
    n_tab = cos_t.shape[0] // tm
    n_pages = page_table.shape[1]
    n_past = n_pages // PAGES_PER_BLOCK
    n_pg = SMP_G * PAGES_PER_BLOCK
    smp_steps = n_past // SMP_G
    n_grid = n_batch * smp_steps
    assert n_grid >= n_proj
    pt = page_table.reshape(-1)
    ck = cache_k.reshape(cache_k.shape[0], PAGE_SIZE * H_ATT, HD_ATT)
    cv = cache_v.reshape(cache_v.shape[0], PAGE_SIZE * H_ATT, HD_ATT)

    prow = lambda t: jnp.minimum(t, n_proj - 1) // n_col
    pcol = lambda t: jnp.minimum(t, n_proj - 1) % n_col

    def page_spec(jj):
        return pl.BlockSpec(
            (None, PAGE_SIZE * H_ATT, HD_ATT),
            lambda t, pt_ref: (pt_ref[(t // smp_steps) * n_pages + (t % smp_steps) * n_pg + jj], 0, 0))

    srow = lambda col: pl.BlockSpec((n_tok, ATT_W), lambda t, pt_ref: (t // smp_steps, col))
    rows = H_ATT * n_tok
    pg_rows = PAGE_SIZE * H_ATT
    row_head = jnp.arange(rows, dtype=jnp.int32)[:, None] // n_tok
    col_head = jnp.arange(SMP_SUB, dtype=jnp.int32)[None, :] % H_ATT
    head_bias = jnp.where(row_head == col_head, 0.0, NEG_BIG).astype(F32)
    n_parts = n_past * pg_rows // SMP_SUB * PAGES_PER_BLOCK
    grid_spec = pltpu.PrefetchScalarGridSpec(
        num_scalar_prefetch=1,
        grid=(n_grid,),
        in_specs=[
            pl.BlockSpec((tm, D_MODEL), lambda t, pt_ref: (prow(t), 0), pipeline_mode=pl.Buffered(1)),
            pl.BlockSpec((D_MODEL, IN_STEP), lambda t, pt_ref: (0, pcol(t))),
            pl.BlockSpec((tm, DK_RET // 2), lambda t, pt_ref: (prow(t) % n_tab, 0)),
            pl.BlockSpec((tm, DK_RET // 2), lambda t, pt_ref: (prow(t) % n_tab, 0)),
            srow(COL_QA // ATT_W), srow(0), srow(0),
            pl.BlockSpec((rows, SMP_SUB), lambda t, pt_ref: (0, 0)),
        ] + [page_spec(jj) for jj in range(n_pg)] + [page_spec(jj) for jj in range(n_pg)],
        out_specs=[
            pl.BlockSpec((tm, IN_STEP), lambda t, pt_ref: (prow(t), pcol(t))),
            pl.BlockSpec((tm, ATT_W), lambda t, pt_ref: (prow(t), 0)),
            pl.BlockSpec((tm, ATT_W), lambda t, pt_ref: (prow(t), 0)),
            srow(0),
        ],
        scratch_shapes=[
            pltpu.VMEM((tm, D_MODEL), BF16),
            pltpu.VMEM((rows, 128), F32),
            pltpu.VMEM((rows, 128), F32),
            pltpu.VMEM((rows, 128), F32),
            pltpu.VMEM((n_parts, rows, HD_ATT), F32),
            pltpu.VMEM((128, HD_ATT), F32),
            pltpu.VMEM((128, HD_ATT), F32),
        ],
    )
    return pl.pallas_call(
        functools.partial(_proj_moba_kernel, n_proj=n_proj, n_col=n_col, smp_steps=smp_steps,
                          n_tok=n_tok, n_past=n_past),
        grid_spec=grid_spec,
        out_shape=[
            jax.ShapeDtypeStruct((m, W_IN_COLS), BF16),
            jax.ShapeDtypeStruct((m, ATT_W), F32),
            jax.ShapeDtypeStruct((m, ATT_W), F32),
            jax.ShapeDtypeStruct((z_s.shape[0], ATT_W), z_s.dtype),
        ],
        compiler_params=_params(("arbitrary",)),
        name="proj_moba",
    )(pt, x, w_bf, cos_t, sin_t, z_s, k_s, v_s, head_bias, *([ck] * n_pg), *([cv] * n_pg))


MIX_TN = 1024
DOT_TN = 512


def _gate_mix_kernel(on_ref, oa_ref, wr_ref, wa_ref, gbr_ref, gba_ref, o_ref):
    on = on_ref[...].astype(BF16)
    oa = oa_ref[...].astype(BF16)
    for t in range(MIX_TN // DOT_TN):
        sl = slice(t * DOT_TN, (t + 1) * DOT_TN)
        ret = _dot(on, wr_ref[:, sl])
        att = _dot(oa, wa_ref[:, sl])
        mix = _sigmoid(gbr_ref[:, sl].astype(F32)) * ret + _sigmoid(gba_ref[:, sl].astype(F32)) * att
        o_ref[:, sl] = mix.astype(o_ref.dtype)


def _gate_mix(on, oa, z, w_ret, w_att, tm):
    m = on.shape[0]
    jr = COL_GBR // MIX_TN
    ja = COL_GBA // MIX_TN
    return pl.pallas_call(
        _gate_mix_kernel,
        grid=(m // tm, D_MODEL // MIX_TN),
        in_specs=[
            pl.BlockSpec((tm, RET_V), lambda i, j: (i, 0)),
            pl.BlockSpec((tm, ATT_W), lambda i, j: (i, 0)),
            pl.BlockSpec((RET_V, MIX_TN), lambda i, j: (0, j)),
            pl.BlockSpec((ATT_W, MIX_TN), lambda i, j: (0, j)),
            pl.BlockSpec((tm, MIX_TN), lambda i, j: (i, jr + j)),
            pl.BlockSpec((tm, MIX_TN), lambda i, j: (i, ja + j)),
        ],
        out_specs=pl.BlockSpec((tm, MIX_TN), lambda i, j: (i, j)),
        out_shape=jax.ShapeDtypeStruct((m, D_MODEL), BF16),
        compiler_params=_params(("arbitrary", "arbitrary")),
        name="gate_mix",
    )(on, oa, w_ret, w_att, z, z)


OUT_ROWS = 256


def _out_ln_kernel(mix_ref, w_ref, x_ref, g_ref, b_ref, h_ref, hb_ref):
    for r in range(mix_ref.shape[0] // OUT_ROWS):
        rs = slice(r * OUT_ROWS, (r + 1) * OUT_ROWS)
        mix = _dot(mix_ref[rs, :], w_ref[...])
        h = _layer_norm_rows(ALPHA * x_ref[rs, :] + mix, g_ref[...], b_ref[...])
        h_ref[rs, :] = h
        hb_ref[rs, :] = h.astype(BF16)


def _out_ln(mix_in, w_out, x, g, b, tm):
    m = x.shape[0]
    rowspec = pl.BlockSpec((tm, D_MODEL), lambda i: (i, 0))
    vec = pl.BlockSpec((1, D_MODEL), lambda i: (0, 0))
    return pl.pallas_call(
        _out_ln_kernel,
        grid=(m // tm,),
        in_specs=[rowspec, pl.BlockSpec((D_MODEL, D_MODEL), lambda i: (0, 0)), rowspec, vec, vec],
        out_specs=[rowspec, rowspec],
        out_shape=[jax.ShapeDtypeStruct((m, D_MODEL), F32), jax.ShapeDtypeStruct((m, D_MODEL), BF16)],
        compiler_params=_params(("arbitrary",)),
        name="out_ln",
    )(mix_in, w_out, x, g, b)


MLP_TF = 1024


def _mlp_ln_kernel(h_ref, hb_ref, wu_ref, wd_ref, g_ref, b_ref, y_ref, acc_ref):
    f = pl.program_id(1)

    @pl.when(f == 0)
    def _():
        acc_ref[...] = ALPHA * h_ref[...]

    hb = hb_ref[...]
    pieces = []
    for c in range(MLP_TF // DOT_TN):
        a = jnp.maximum(_dot(hb, wu_ref[:, c * DOT_TN:(c + 1) * DOT_TN]), 0.0)
        pieces.append((a * a).astype(BF16))
    act = jnp.concatenate(pieces, axis=1)
    for n in range(D_MODEL // DOT_TN):
        sl = slice(n * DOT_TN, (n + 1) * DOT_TN)
        acc_ref[:, sl] += _dot(act, wd_ref[:, sl])

    @pl.when(f == pl.num_programs(1) - 1)
    def _():
        y_ref[...] = _layer_norm_rows(acc_ref[...], g_ref[...], b_ref[...])


def _mlp_ln(h, hb, w_up, w_down, g, b, tm):
    m = h.shape[0]
    rowspec = pl.BlockSpec((tm, D_MODEL), lambda i, f: (i, 0))
    vec = pl.BlockSpec((1, D_MODEL), lambda i, f: (0, 0))
    return pl.pallas_call(
        _mlp_ln_kernel,
        grid=(m // tm, D_FF // MLP_TF),
        in_specs=[
            rowspec, rowspec,
            pl.BlockSpec((D_MODEL, MLP_TF), lambda i, f: (0, f)),
            pl.BlockSpec((MLP_TF, D_MODEL), lambda i, f: (f, 0)),
            vec, vec,
        ],
        out_specs=rowspec,
        out_shape=jax.ShapeDtypeStruct((m, D_MODEL), F32),
        scratch_shapes=[pltpu.VMEM((tm, D_MODEL), F32)],
        compiler_params=_params(("arbitrary", "arbitrary")),
        name="mlp_ln",
    )(h, hb, w_up, w_down, g, b)


def _rotary_tables(pos):
    half = DK_RET // 2
    inv = ROPE_BASE ** (-jnp.arange(half, dtype=F32) / half)
    ang = pos.astype(F32)[:, None] * inv[None, :]
    return jnp.cos(ang), jnp.sin(ang)


def _merge(x, z, on, oa, w, tm):
    mix_in = _gate_mix(on, oa, z, w["ret"], w["att"], tm)
    h, hb = _out_ln(mix_in, w["out"], x, w["ln1_g"], w["ln1_b"], tm)
    return _mlp_ln(h, hb, w["up"], w["down"], w["ln2_g"], w["ln2_b"], tm)


def kernel(x_prompt, x_sample, cache_k, cache_v, state_ret, page_table, w_in, ret_gn_gain, w_ret_br,
           w_att_br, w_out, ln1_g, ln1_b, w_up, w_down, ln2_g, ln2_b):
    n_b, seq, _ = x_prompt.shape
    n_db, n_tok, _ = x_sample.shape
    assert w_in.shape[0] == DEPTH == 1
    l = 0
    w = {
        "in": w_in[l].astype(BF16), "ret": w_ret_br[l].astype(BF16), "att": w_att_br[l].astype(BF16),
        "out": w_out[l].astype(BF16), "up": w_up[l].astype(BF16), "down": w_down[l].astype(BF16),
        "ln1_g": ln1_g[l][None, :], "ln1_b": ln1_b[l][None, :],
        "ln2_g": ln2_g[l][None, :], "ln2_b": ln2_b[l][None, :],
    }
    gain = ret_gn_gain[l][:, None, :]

    xs = x_sample.reshape(n_db * n_tok, D_MODEL)
    pos_s = PAST_LEN + jnp.arange(n_tok, dtype=jnp.int32)
    cos_s, sin_s = _rotary_tables(jnp.tile(pos_s, n_db))
    tm_s = n_db * n_tok
    z_s, k_s, v_s = _in_proj(xs, w["in"], cos_s, sin_s, tm_s, F32)
    xp = x_prompt.reshape(n_b * seq, D_MODEL)
    cos_p, sin_p = _rotary_tables(jnp.arange(seq, dtype=jnp.int32))
    z_p, k_p, v_p, oa_s = _in_proj_with_moba_sample(
        xp, w["in"], cos_p, sin_p, 512, z_s, k_s, v_s, cache_k[l], cache_v[l], page_table, n_db, n_tok)

    on_p, s_p = _retention(z_p, None, _retention_tables(RET_CHUNK), gain, n_b, seq // RET_CHUNK,
                           RET_CHUNK, BF16, n_heads=4)
    oa_p = _moba_prompt(z_p, k_p, v_p, n_b, seq)
    y_p = _merge(xp, z_p, on_p, oa_p, w, 512)

    on_s, s_s = _retention(z_s, state_ret[l], _retention_tables(n_tok), gain, n_db, 1, n_tok, F32,
                           n_heads=H_RET)
    y_s = _merge(xs, z_s, on_s, oa_s, w, tm_s)

    return (
        y_p.reshape(n_b, seq, D_MODEL),
        y_s.reshape(n_db, n_tok, D_MODEL),
        k_p.reshape(1, n_b, seq, H_ATT, HD_ATT),
        v_p.reshape(1, n_b, seq, H_ATT, HD_ATT),
        s_p[None],
        k_s.reshape(1, n_db, n_tok, H_ATT, HD_ATT),
        v_s.reshape(1, n_db, n_tok, H_ATT, HD_ATT),
        s_s[None],
    )
```

```python
import functools
import math

import jax
import jax.numpy as jnp
from jax import lax
from jax.experimental import pallas as pl
from jax.experimental.pallas import tpu as pltpu

F32 = jnp.float32
BF16 = jnp.bfloat16

D_MODEL = 2048
DEPTH = 1
PAST_LEN = 8192
PAGE_SIZE = 128
H_RET = 8
DK_RET = 256
DV_RET = 256
RET_CHUNK = 128
H_ATT = 8
HD_ATT = 128
MOBA_BLOCK = 256
MOBA_TOPK = 3
D_FF = 4 * D_MODEL
RET_QK = H_RET * DK_RET
RET_V = H_RET * DV_RET
ATT_W = H_ATT * HD_ATT
W_IN_COLS = 2 * RET_QK + 2 * RET_V + 3 * ATT_W + 2 * D_MODEL
ROPE_BASE = 10000.0
LN_EPS = 1e-5
GN_EPS = 1e-5
ALPHA = (2 * DEPTH) ** 0.25

COL_QR = 0
COL_KR = RET_QK
COL_VR = 2 * RET_QK
COL_GR = 2 * RET_QK + RET_V
COL_QA = 2 * RET_QK + 2 * RET_V
COL_KA = COL_QA + ATT_W
COL_VA = COL_KA + ATT_W
COL_GBR = COL_VA + ATT_W
COL_GBA = COL_GBR + D_MODEL

NEG_BIG = -1e30
SOFTMAX_C = HD_ATT ** -0.5 * math.log2(math.e)
PAGES_PER_BLOCK = MOBA_BLOCK // PAGE_SIZE
VMEM_LIMIT = 56 * 1024 * 1024


def _params(semantics):
    return pltpu.CompilerParams(dimension_semantics=semantics, vmem_limit_bytes=VMEM_LIMIT)


def _sigmoid(x):
    return 1.0 / (1.0 + jnp.exp(-x))


def _dot(a, b):
    return jnp.dot(a, b, preferred_element_type=F32)


def _dot_nt(a, b):
    return lax.dot_general(a, b, (((1,), (1,)), ((), ())), preferred_element_type=F32)


def _dot_tn(a, b):
    return lax.dot_general(a, b, (((0,), (0,)), ((), ())), preferred_element_type=F32)


def _layer_norm_rows(x, g, b):
    mu = jnp.mean(x, axis=-1, keepdims=True)
    xc = x - mu
    var = jnp.mean(xc * xc, axis=-1, keepdims=True)
    return xc * lax.rsqrt(var + LN_EPS) * g + b


IN_TN = 512
IN_STEP = 1024
_J_ROT_END = COL_VR // IN_STEP
_J_KR = COL_KR // IN_STEP
_J_KA = COL_KA // IN_STEP
_J_VA = COL_VA // IN_STEP
assert ATT_W == IN_STEP and COL_VR % IN_STEP == 0 and COL_QA % IN_STEP == 0


def _in_proj_kernel(*refs):
    _in_proj_body(pl.program_id(1), None, *refs)


def _in_proj_body(j, active, x_ref, w_ref, cos_ref, sin_ref, z_ref, k_ref, v_ref, xb_ref):
    tiles = [slice(t * IN_TN, (t + 1) * IN_TN) for t in range(IN_STEP // IN_TN)]
    when = lambda cond: pl.when(cond if active is None else active & cond)

    if xb_ref is None:
        lhs_ref = x_ref
    else:
        lhs_ref = xb_ref

        @when(j == 0)
        def _():
            xb_ref[...] = x_ref[...].astype(BF16)

    def tile_dot(sl):
        return _dot(lhs_ref[...], w_ref[:, sl])

    @when(j < _J_ROT_END)
    def _():
        cos = cos_ref[...]
        sin = sin_ref[...]
        scale = jnp.where(j >= _J_KR, DK_RET ** -0.5, 1.0).astype(F32)
        half = DK_RET // 2
        for sl in tiles:
            acc = tile_dot(sl)
            for hh in range(IN_TN // DK_RET):
                c0 = hh * DK_RET
                o0 = sl.start + c0
                x1 = acc[:, c0:c0 + half]
                x2 = acc[:, c0 + half:c0 + DK_RET]
                z_ref[:, o0:o0 + half] = ((x1 * cos - x2 * sin) * scale).astype(z_ref.dtype)
                z_ref[:, o0 + half:o0 + DK_RET] = ((x1 * sin + x2 * cos) * scale).astype(z_ref.dtype)

    @when((j >= _J_ROT_END) & (j != _J_KA) & (j != _J_VA))
    def _():
        for sl in tiles:
            z_ref[:, sl] = tile_dot(sl).astype(z_ref.dtype)

    def new_rows(out_ref):
        for sl in tiles:
            acc = tile_dot(sl)
            z_ref[:, sl] = acc.astype(z_ref.dtype)
            out_ref[:, sl] = acc

    when(j == _J_KA)(lambda: new_rows(k_ref))
    when(j == _J_VA)(lambda: new_rows(v_ref))


def _in_proj(x, w_bf, cos_t, sin_t, tm, z_dtype):
    m = x.shape[0]
    n_tab = cos_t.shape[0] // tm
    return pl.pallas_call(
        _in_proj_kernel,
        grid=(m // tm, W_IN_COLS // IN_STEP),
        in_specs=[
            pl.BlockSpec((tm, D_MODEL), lambda i, j: (i, 0), pipeline_mode=pl.Buffered(1)),
            pl.BlockSpec((D_MODEL, IN_STEP), lambda i, j: (0, j)),
            pl.BlockSpec((tm, DK_RET // 2), lambda i, j: (i % n_tab, 0)),
            pl.BlockSpec((tm, DK_RET // 2), lambda i, j: (i % n_tab, 0)),
        ],
        out_specs=[
            pl.BlockSpec((tm, IN_STEP), lambda i, j: (i, j)),
            pl.BlockSpec((tm, ATT_W), lambda i, j: (i, 0)),
            pl.BlockSpec((tm, ATT_W), lambda i, j: (i, 0)),
        ],
        out_shape=[
            jax.ShapeDtypeStruct((m, W_IN_COLS), z_dtype),
            jax.ShapeDtypeStruct((m, ATT_W), F32),
            jax.ShapeDtypeStruct((m, ATT_W), F32),
        ],
        scratch_shapes=[pltpu.VMEM((tm, D_MODEL), BF16)],
        compiler_params=_params(("arbitrary", "arbitrary")),
        name="in_proj",
    )(x, w_bf, cos_t, sin_t)


def _retention_kernel(*refs, c_data, has_init, n_heads):
    if has_init:
        (q_ref, k_ref, v_ref, g_ref, s0_ref, dec_ref, qd_ref, kd_ref, gc_ref, gain_ref,
         on_ref, s_ref, pad_ref) = refs
    else:
        (q_ref, k_ref, v_ref, g_ref, dec_ref, qd_ref, kd_ref, gc_ref, gain_ref,
         on_ref, s_ref, pad_ref) = refs
        s0_ref = None
    c = pl.program_id(2)
    padded = c_data != RET_CHUNK

    @pl.when(c == 0)
    def _():
        if has_init:
            s_ref[...] = s0_ref[...]
        else:
            s_ref[...] = jnp.zeros(s_ref.shape, F32)

    def load(ref, slot):
        if not padded:
            return ref[...]
        pad_ref[slot] = jnp.zeros(pad_ref.shape[1:], F32)
        pad_ref[slot, 0:c_data, :] = ref[...].astype(F32)
        return pad_ref[slot]

    q_all = load(q_ref, 0)
    k_all = load(k_ref, 1)
    v_all = load(v_ref, 2)
    g_all = g_ref[...]

    for hh in range(n_heads):
        lo, hi = hh * DK_RET, (hh + 1) * DK_RET
        q = q_all[:, lo:hi].astype(BF16)
        k = k_all[:, lo:hi]
        v = v_all[:, lo:hi].astype(BF16)
        s_old = s_ref[0, hh]
        scores = _dot_nt(q, k.astype(BF16)) * dec_ref[hh]
        inner = _dot(scores.astype(BF16), v)
        cross = _dot(q, s_old.astype(BF16)) * qd_ref[hh]
        o = inner + cross
        kd = (k.astype(F32) * kd_ref[hh]).astype(BF16)
        s_ref[0, hh] = gc_ref[hh] * s_old + _dot_tn(kd, v)
        if padded:
            o = o[0:c_data]
        mu = jnp.mean(o, axis=-1, keepdims=True)
        oc = o - mu
        var = jnp.mean(oc * oc, axis=-1, keepdims=True)
        o_n = oc * lax.rsqrt(var + GN_EPS) * gain_ref[hh]
        g = g_all[:, lo:hi].astype(F32)
        on_ref[:, lo:hi] = (o_n * (g * _sigmoid(g))).astype(on_ref.dtype)


def _retention(z, s0, tabs, gain, n_batch, n_chunks, c_data, out_dtype, n_heads):
    dec, qd, kd, gc = tabs
    m = z.shape[0]
    wblk = DK_RET * n_heads
    has_init = s0 is not None

    def zspec(col0):
        base = col0 // wblk
        return pl.BlockSpec((c_data, wblk), lambda b, hg, c: (b * n_chunks + c, base + hg))

    tab3 = lambda shape: pl.BlockSpec((n_heads,) + shape, lambda b, hg, c: (hg, 0, 0))
    sspec = pl.BlockSpec((1, n_heads, DK_RET, DV_RET), lambda b, hg, c: (b, hg, 0, 0))
    in_specs = [zspec(COL_QR), zspec(COL_KR), zspec(COL_VR), zspec(COL_GR)]
    args = [z, z, z, z]
    if has_init:
        in_specs.append(sspec)
        args.append(s0)
    in_specs += [tab3((RET_CHUNK, RET_CHUNK)), tab3((RET_CHUNK, 1)), tab3((RET_CHUNK, 1)),
                 tab3((1, 1)), tab3((1, DV_RET))]
    args += [dec, qd, kd, gc, gain]
    return pl.pallas_call(
        functools.partial(_retention_kernel, c_data=c_data, has_init=has_init, n_heads=n_heads),
        grid=(n_batch, H_RET // n_heads, n_chunks),
        in_specs=in_specs,
        out_specs=[
            pl.BlockSpec((c_data, wblk), lambda b, hg, c: (b * n_chunks + c, hg)),
            sspec,
        ],
        out_shape=[
            jax.ShapeDtypeStruct((m, RET_V), out_dtype),
            jax.ShapeDtypeStruct((n_batch, H_RET, DK_RET, DV_RET), F32),
        ],
        scratch_shapes=[pltpu.VMEM((3, RET_CHUNK, wblk), F32)],
        compiler_params=_params(("arbitrary", "arbitrary", "arbitrary")),
        name="retention",
    )(*args)


def _retention_tables(c_data):
    log_gamma = jnp.log1p(-jnp.exp2(-5.0 - jnp.arange(H_RET, dtype=F32)))
    idx = jnp.arange(RET_CHUNK, dtype=F32)
    live = idx < c_data
    diff = idx[:, None] - idx[None, :]
    causal = (diff >= 0) & live[:, None] & live[None, :]
    dec = jnp.where(causal, jnp.exp(log_gamma[:, None, None] * jnp.where(causal, diff, 0.0)), 0.0)
    qd = jnp.exp(log_gamma[:, None] * (idx[None, :] + 1.0))
    kd = jnp.where(live[None, :], jnp.exp(log_gamma[:, None] * (c_data - 1.0 - idx[None, :])), 0.0)
    gc = jnp.exp(log_gamma * c_data)
    return dec, qd[:, :, None], kd[:, :, None], gc[:, None, None]


def _moba_prompt_kernel(q_ref, k_ref, v_ref, o_ref, kb_ref, vb_ref, km_ref):
    nb = k_ref.shape[0] // MOBA_BLOCK
    blk = MOBA_BLOCK

    kb_ref[...] = k_ref[...].astype(BF16)
    vb_ref[...] = v_ref[...].astype(BF16)
    km_ref[...] = jnp.zeros(km_ref.shape, F32)
    for n in range(nb):
        km_ref[n:n + 1, :] = jnp.sum(k_ref[n * blk:(n + 1) * blk, :], axis=0, keepdims=True) * (1.0 / blk)
    kmb = km_ref[...].astype(BF16)

    lane = lax.broadcasted_iota(jnp.int32, (blk, 128), 1)
    row = lax.broadcasted_iota(jnp.int32, (blk, blk), 0)
    col = lax.broadcasted_iota(jnp.int32, (blk, blk), 1)
    causal_bias = jnp.where(col <= row, 0.0, NEG_BIG).astype(F32)

    for i in range(nb):
        q = q_ref[i * blk:(i + 1) * blk, :].astype(BF16)
        pieces = []
        if i > 0:
            s_el = jnp.where(lane < i, _dot_nt(q, kmb), -jnp.inf)
            cnt = jnp.zeros(s_el.shape, jnp.int32)
            for n in range(i):
                cn = s_el[:, n:n + 1]
                ahead = (cn > s_el) | ((cn == s_el) & (lane > n))
                cnt = cnt + ahead.astype(jnp.int32)
            bias = jnp.where((lane < i) & (cnt < MOBA_TOPK), 0.0, NEG_BIG).astype(F32)
            pieces = [jnp.broadcast_to(bias[:, n:n + 1], (blk, blk)) for n in range(i)]
        nk = (i + 1) * blk
        s = _dot_nt(q, kb_ref[0:nk, :]) + jnp.concatenate(pieces + [causal_bias], axis=1)
        m = jnp.max(s, axis=-1, keepdims=True)
        p = jnp.exp2((s - m) * SOFTMAX_C)
        l = jnp.sum(p, axis=-1, keepdims=True)
        o = _dot(p.astype(BF16), vb_ref[0:nk, :]) / l
        o_ref[i * blk:(i + 1) * blk, :] = o.astype(o_ref.dtype)


def _moba_prompt(z, k_new, v_new, n_batch, seq):
    m = z.shape[0]
    qcol = COL_QA // HD_ATT
    head_rows = lambda col0: pl.BlockSpec((seq, HD_ATT), lambda b, h: (b, col0 + h))
    return pl.pallas_call(
        _moba_prompt_kernel,
        grid=(n_batch, H_ATT),
        in_specs=[head_rows(qcol), head_rows(0), head_rows(0)],
        out_specs=head_rows(0),
        out_shape=jax.ShapeDtypeStruct((m, ATT_W), z.dtype),
        scratch_shapes=[
            pltpu.VMEM((seq, HD_ATT), BF16),
            pltpu.VMEM((seq, HD_ATT), BF16),
            pltpu.VMEM((128, HD_ATT), F32),
        ],
        compiler_params=_params(("arbitrary", "arbitrary")),
        name="moba_prompt",
    )(z, k_new, v_new)


SMP_G = 4
SMP_SUB = 512


def _moba_sample_body(s_id, n_steps, q_ref, kn_ref, vn_ref, hb_ref, ck, cv,
                      o_ref, m_all, l_all, s_all, o_sc, kpad, vpad, *, n_tok, n_past):
    rows = H_ATT * n_tok
    tok_shift = n_tok.bit_length() - 1
    assert n_tok == 1 << tok_shift and H_ATT & (H_ATT - 1) == 0 and rows <= 128

    @pl.when(s_id == 0)
    def _():
        m_all[...] = jnp.zeros(m_all.shape, F32)
        l_all[...] = jnp.zeros(l_all.shape, F32)
        s_all[...] = jnp.zeros(s_all.shape, F32)

    def by_head(ref):
        x = ref[...]
        return jnp.concatenate([x[:, h * HD_ATT:(h + 1) * HD_ATT] for h in range(H_ATT)], axis=0)

    qb = by_head(q_ref).astype(BF16)
    qr = qb.astype(F32)
    hb = hb_ref[...]
    lane = lax.broadcasted_iota(jnp.int32, (rows, 128), 1)
    subs = [slice(t * SMP_SUB, (t + 1) * SMP_SUB) for t in range(PAGE_SIZE * H_ATT // SMP_SUB)]
    ppb = PAGES_PER_BLOCK * len(subs)
    ppb_shift = ppb.bit_length() - 1
    n_parts = n_past * ppb
    assert ppb == 1 << ppb_shift and n_parts <= 128

    pieces = []
    for g in range(SMP_G):
        for pg in range(PAGES_PER_BLOCK):
            for t, rs in enumerate(subs):
                pieces.append((g * ppb + pg * len(subs) + t, g, PAGES_PER_BLOCK * g + pg, rs))
    first = s_id * (SMP_G * ppb)

    ksum = [jnp.zeros((H_ATT, HD_ATT), F32) for _ in range(SMP_G)]
    scores = []
    for _, g, pi, rs in pieces:
        kt = ck[pi][rs, :]
        ksum[g] = ksum[g] + jnp.sum(kt.reshape(SMP_SUB // H_ATT, H_ATT, HD_ATT), axis=0)
        scores.append(_dot_nt(qb, kt.astype(BF16)))
    probs, stats = [], []
    for s in scores:
        s = s + hb
        m_t = jnp.max(s, axis=-1, keepdims=True)
        p = jnp.exp2((s - m_t) * SOFTMAX_C)
        probs.append(p.astype(BF16))
        stats.append((m_t, jnp.sum(p, axis=-1, keepdims=True)))
    for (local, _, pi, rs), p in zip(pieces, probs):
        o_sc[first + local] = _dot(p, cv[pi][rs, :].astype(BF16))

    m_new, l_new, s_new = m_all[...], l_all[...], s_all[...]
    for g in range(SMP_G):
        kmean = (ksum[g] * (1.0 / MOBA_BLOCK)).astype(BF16).astype(F32)
        kexp = jnp.concatenate(
            [jnp.broadcast_to(kmean[h:h + 1, :], (n_tok, HD_ATT)) for h in range(H_ATT)], axis=0)
        sc = jnp.sum(qr * kexp, axis=-1, keepdims=True)
        for (local, gg, _, _), (m_t, l_t) in zip(pieces, stats):
            if gg == g:
                here = lane == first + local
                m_new = jnp.where(here, m_t, m_new)
                l_new = jnp.where(here, l_t, l_new)
                s_new = jnp.where(here, sc, s_new)
    m_all[...] = m_new
    l_all[...] = l_new
    s_all[...] = s_new

    @pl.when(s_id == n_steps - 1)
    def _():
        past = lane < n_parts
        lane_blk = lane >> ppb_shift
        sc_all = jnp.where(past, s_all[...], -jnp.inf)
        cnt = jnp.zeros(sc_all.shape, jnp.int32)
        for nn in range(n_past):
            col = sc_all[:, nn * ppb:nn * ppb + 1]
            ahead = (col > sc_all) | ((col == sc_all) & (lane_blk > nn))
            cnt = cnt + ahead.astype(jnp.int32)
        sel = past & (cnt < MOBA_TOPK)
        m_sel = jnp.where(sel, m_all[...], NEG_BIG)

        kpad[...] = jnp.zeros(kpad.shape, F32)
        vpad[...] = jnp.zeros(vpad.shape, F32)
        kpad[0:rows, :] = by_head(kn_ref)
        vpad[0:rows, :] = by_head(vn_ref)
        r_id = lax.broadcasted_iota(jnp.int32, (rows, 128), 0)
        own_ok = ((lane < rows) & ((lane >> tok_shift) == (r_id >> tok_shift))
                  & ((lane & (n_tok - 1)) <= (r_id & (n_tok - 1))))
        s_own = jnp.where(own_ok, _dot_nt(qb, kpad[...].astype(BF16)), NEG_BIG)
        big = jnp.maximum(jnp.max(m_sel, axis=-1, keepdims=True), jnp.max(s_own, axis=-1, keepdims=True))
        w = jnp.where(sel, jnp.exp2((m_sel - big) * SOFTMAX_C), 0.0)
        p_own = jnp.exp2((s_own - big) * SOFTMAX_C)
        den = jnp.sum(w * l_all[...], axis=-1, keepdims=True) + jnp.sum(p_own, axis=-1, keepdims=True)
        num = _dot(p_own.astype(BF16), vpad[...].astype(BF16))
        for pp in range(n_parts):
            num = num + w[:, pp:pp + 1] * o_sc[pp]
        out = num / den
        for h in range(H_ATT):
            o_ref[:, h * HD_ATT:(h + 1) * HD_ATT] = out[h * n_tok:(h + 1) * n_tok, :].astype(o_ref.dtype)


def _proj_moba_kernel(pt_ref, x_ref, w_ref, cos_ref, sin_ref, q_ref, kn_ref, vn_ref, hb_ref, *rest,
                      n_proj, n_row, smp_steps, n_tok, n_past):
    n_pg = SMP_G * PAGES_PER_BLOCK
    ck = rest[:n_pg]
    cv = rest[n_pg:2 * n_pg]
    z_ref, k_ref, v_ref, o_ref, m_all, l_all, s_all, o_sc, kpad, vpad = rest[2 * n_pg:]
    t = pl.program_id(0)
    _in_proj_body(jnp.minimum(t, n_proj - 1) // n_row, t < n_proj,
                  x_ref, w_ref, cos_ref, sin_ref, z_ref, k_ref, v_ref, None)
    _moba_sample_body(t % smp_steps, smp_steps, q_ref, kn_ref, vn_ref, hb_ref, ck, cv,
                      o_ref, m_all, l_all, s_all, o_sc, kpad, vpad, n_tok=n_tok, n_past=n_past)


def _in_proj_with_moba_sample(x, w_bf, cos_t, sin_t, tm, z_s, k_s, v_s, cache_k, cache_v, page_table,
                              n_batch, n_tok):
    assert x.dtype == BF16
    m = x.shape[0]
    n_col = W_IN_COLS // IN_STEP
    n_row = m // tm
    n_proj = n_row * n_col
    n_tab = cos_t.shape[0] // tm
    n_pages = page_table.shape[1]
    n_past = n_pages // PAGES_PER_BLOCK
    n_pg = SMP_G * PAGES_PER_BLOCK
    smp_steps = n_past // SMP_G
    n_grid = n_batch * smp_steps
    assert n_grid >= n_proj
    pt = page_table.reshape(-1)
    ck = cache_k.reshape(cache_k.shape[0], PAGE_SIZE * H_ATT, HD_ATT)
    cv = cache_v.reshape(cache_v.shape[0], PAGE_SIZE * H_ATT, HD_ATT)

    pcol = lambda t: jnp.minimum(t, n_proj - 1) // n_row
    prow = lambda t: jnp.minimum(t, n_proj - 1) % n_row

    def new_rows_spec(j_write):
        def index(t, pt_ref):
            j = pcol(t)
            return (jnp.where(j < j_write, 0, jnp.where(j == j_write, prow(t), n_row - 1)), 0)
        return pl.BlockSpec((tm, ATT_W), index)

    def page_spec(jj):
        return pl.BlockSpec(
            (None, PAGE_SIZE * H_ATT, HD_ATT),
            lambda t, pt_ref: (pt_ref[(t // smp_steps) * n_pages + (t % smp_steps) * n_pg + jj], 0, 0))

    srow = lambda col: pl.BlockSpec((n_tok, ATT_W), lambda t, pt_ref: (t // smp_steps, col))
    rows = H_ATT * n_tok
    pg_rows = PAGE_SIZE * H_ATT
    row_head = jnp.arange(rows, dtype=jnp.int32)[:, None] // n_tok
    col_head = jnp.arange(SMP_SUB, dtype=jnp.int32)[None, :] % H_ATT
    head_bias = jnp.where(row_head == col_head, 0.0, NEG_BIG).astype(F32)
    n_parts = n_past * pg_rows // SMP_SUB * PAGES_PER_BLOCK
    grid_spec = pltpu.PrefetchScalarGridSpec(
        num_scalar_prefetch=1,
        grid=(n_grid,),
        in_specs=[
            pl.BlockSpec((tm, D_MODEL), lambda t, pt_ref: (prow(t), 0)),
            pl.BlockSpec((D_MODEL, IN_STEP), lambda t, pt_ref: (0, pcol(t))),
            pl.BlockSpec((tm, DK_RET // 2), lambda t, pt_ref: (prow(t) % n_tab, 0)),
            pl.BlockSpec((tm, DK_RET // 2), lambda t, pt_ref: (prow(t) % n_tab, 0)),
            srow(COL_QA // ATT_W), srow(0), srow(0),
            pl.BlockSpec((rows, SMP_SUB), lambda t, pt_ref: (0, 0)),
        ] + [page_spec(jj) for jj in range(n_pg)] + [page_spec(jj) for jj in range(n_pg)],
        out_specs=[
            pl.BlockSpec((tm, IN_STEP), lambda t, pt_ref: (prow(t), pcol(t))),
            new_rows_spec(_J_KA),
            new_rows_spec(_J_VA),
            srow(0),
        ],
        scratch_shapes=[
            pltpu.VMEM((rows, 128), F32),
            pltpu.VMEM((rows, 128), F32),
            pltpu.VMEM((rows, 128), F32),
            pltpu.VMEM((n_parts, rows, HD_ATT), F32),
            pltpu.VMEM((128, HD_ATT), F32),
            pltpu.VMEM((128, HD_ATT), F32),
        ],
    )
    return pl.pallas_call(
        functools.partial(_proj_moba_kernel, n_proj=n_proj, n_row=n_row, smp_steps=smp_steps,
                          n_tok=n_tok, n_past=n_past),
        grid_spec=grid_spec,
        out_shape=[
            jax.ShapeDtypeStruct((m, W_IN_COLS), BF16),
            jax.ShapeDtypeStruct((m, ATT_W), F32),
            jax.ShapeDtypeStruct((m, ATT_W), F32),
            jax.ShapeDtypeStruct((z_s.shape[0], ATT_W), z_s.dtype),
        ],
        compiler_params=_params(("arbitrary",)),
        name="proj_moba",
    )(pt, x, w_bf, cos_t, sin_t, z_s, k_s, v_s, head_bias, *([ck] * n_pg), *([cv] * n_pg))


MIX_TN = 1024
DOT_TN = 512


def _gate_mix_kernel(on_ref, oa_ref, wr_ref, wa_ref, gbr_ref, gba_ref, o_ref):
    on = on_ref[...].astype(BF16)
    oa = oa_ref[...].astype(BF16)
    for t in range(MIX_TN // DOT_TN):
        sl = slice(t * DOT_TN, (t + 1) * DOT_TN)
        ret = _dot(on, wr_ref[:, sl])
        att = _dot(oa, wa_ref[:, sl])
        mix = _sigmoid(gbr_ref[:, sl].astype(F32)) * ret + _sigmoid(gba_ref[:, sl].astype(F32)) * att
        o_ref[:, sl] = mix.astype(o_ref.dtype)


def _gate_mix(on, oa, z, w_ret, w_att, tm):
    m = on.shape[0]
    jr = COL_GBR // MIX_TN
    ja = COL_GBA // MIX_TN
    return pl.pallas_call(
        _gate_mix_kernel,
        grid=(m // tm, D_MODEL // MIX_TN),
        in_specs=[
            pl.BlockSpec((tm, RET_V), lambda i, j: (i, 0)),
            pl.BlockSpec((tm, ATT_W), lambda i, j: (i, 0)),
            pl.BlockSpec((RET_V, MIX_TN), lambda i, j: (0, j)),
            pl.BlockSpec((ATT_W, MIX_TN), lambda i, j: (0, j)),
            pl.BlockSpec((tm, MIX_TN), lambda i, j: (i, jr + j)),
            pl.BlockSpec((tm, MIX_TN), lambda i, j: (i, ja + j)),
        ],
        out_specs=pl.BlockSpec((tm, MIX_TN), lambda i, j: (i, j)),
        out_shape=jax.ShapeDtypeStruct((m, D_MODEL), BF16),
        compiler_params=_params(("arbitrary", "arbitrary")),
        name="gate_mix",
    )(on, oa, w_ret, w_att, z, z)


OUT_ROWS = 256


def _out_ln_kernel(mix_ref, w_ref, x_ref, g_ref, b_ref, h_ref, hb_ref):
    for r in range(mix_ref.shape[0] // OUT_ROWS):
        rs = slice(r * OUT_ROWS, (r + 1) * OUT_ROWS)
        mix = _dot(mix_ref[rs, :], w_ref[...])
        h = _layer_norm_rows(ALPHA * x_ref[rs, :] + mix, g_ref[...], b_ref[...])
        h_ref[rs, :] = h
        hb_ref[rs, :] = h.astype(BF16)


def _out_ln(mix_in, w_out, x, g, b, tm):
    m = x.shape[0]
    rowspec = pl.BlockSpec((tm, D_MODEL), lambda i: (i, 0))
    vec = pl.BlockSpec((1, D_MODEL), lambda i: (0, 0))
    return pl.pallas_call(
        _out_ln_kernel,
        grid=(m // tm,),
        in_specs=[rowspec, pl.BlockSpec((D_MODEL, D_MODEL), lambda i: (0, 0)), rowspec, vec, vec],
        out_specs=[rowspec, rowspec],
        out_shape=[jax.ShapeDtypeStruct((m, D_MODEL), F32), jax.ShapeDtypeStruct((m, D_MODEL), BF16)],
        compiler_params=_params(("arbitrary",)),
        name="out_ln",
    )(mix_in, w_out, x, g, b)


MLP_TF = 1024


def _mlp_ln_kernel(h_ref, hb_ref, wu_ref, wd_ref, g_ref, b_ref, y_ref, acc_ref):
    f = pl.program_id(1)

    @pl.when(f == 0)
    def _():
        acc_ref[...] = ALPHA * h_ref[...]

    hb = hb_ref[...]
    pieces = []
    for c in range(MLP_TF // DOT_TN):
        a = jnp.maximum(_dot(hb, wu_ref[:, c * DOT_TN:(c + 1) * DOT_TN]), 0.0)
        pieces.append((a * a).astype(BF16))
    act = jnp.concatenate(pieces, axis=1)
    for n in range(D_MODEL // DOT_TN):
        sl = slice(n * DOT_TN, (n + 1) * DOT_TN)
        acc_ref[:, sl] += _dot(act, wd_ref[:, sl])

    @pl.when(f == pl.num_programs(1) - 1)
    def _():
        y_ref[...] = _layer_norm_rows(acc_ref[...], g_ref[...], b_ref[...])


def _mlp_ln(h, hb, w_up, w_down, g, b, tm):
    m = h.shape[0]
    rowspec = pl.BlockSpec((tm, D_MODEL), lambda i, f: (i, 0))
    vec = pl.BlockSpec((1, D_MODEL), lambda i, f: (0, 0))
    return pl.pallas_call(
        _mlp_ln_kernel,
        grid=(m // tm, D_FF // MLP_TF),
        in_specs=[
            rowspec, rowspec,
            pl.BlockSpec((D_MODEL, MLP_TF), lambda i, f: (0, f)),
            pl.BlockSpec((MLP_TF, D_MODEL), lambda i, f: (f, 0)),
            vec, vec,
        ],
        out_specs=rowspec,
        out_shape=jax.ShapeDtypeStruct((m, D_MODEL), F32),
        scratch_shapes=[pltpu.VMEM((tm, D_MODEL), F32)],
        compiler_params=_params(("arbitrary", "arbitrary")),
        name="mlp_ln",
    )(h, hb, w_up, w_down, g, b)


def _rotary_tables(pos):
    half = DK_RET // 2
    inv = ROPE_BASE ** (-jnp.arange(half, dtype=F32) / half)
    ang = pos.astype(F32)[:, None] * inv[None, :]
    return jnp.cos(ang), jnp.sin(ang)


def _merge(x, z, on, oa, w, tm):
    mix_in = _gate_mix(on, oa, z, w["ret"], w["att"], min(2 * tm, x.shape[0]))
    h, hb = _out_ln(mix_in, w["out"], x, w["ln1_g"], w["ln1_b"], tm)
    return _mlp_ln(h, hb, w["up"], w["down"], w["ln2_g"], w["ln2_b"], tm)


def kernel(x_prompt, x_sample, cache_k, cache_v, state_ret, page_table, w_in, ret_gn_gain, w_ret_br,
           w_att_br, w_out, ln1_g, ln1_b, w_up, w_down, ln2_g, ln2_b):
    n_b, seq, _ = x_prompt.shape
    n_db, n_tok, _ = x_sample.shape
    assert w_in.shape[0] == DEPTH == 1
    l = 0
    w = {
        "in": w_in[l].astype(BF16), "ret": w_ret_br[l].astype(BF16), "att": w_att_br[l].astype(BF16),
        "out": w_out[l].astype(BF16), "up": w_up[l].astype(BF16), "down": w_down[l].astype(BF16),
        "ln1_g": ln1_g[l][None, :], "ln1_b": ln1_b[l][None, :],
        "ln2_g": ln2_g[l][None, :], "ln2_b": ln2_b[l][None, :],
    }
    gain = ret_gn_gain[l][:, None, :]

    xs = x_sample.reshape(n_db * n_tok, D_MODEL)
    pos_s = PAST_LEN + jnp.arange(n_tok, dtype=jnp.int32)
    cos_s, sin_s = _rotary_tables(jnp.tile(pos_s, n_db))
    tm_s = n_db * n_tok
    z_s, k_s, v_s = _in_proj(xs, w["in"], cos_s, sin_s, tm_s, F32)
    xp = x_prompt.reshape(n_b * seq, D_MODEL)
    cos_p, sin_p = _rotary_tables(jnp.arange(seq, dtype=jnp.int32))
    z_p, k_p, v_p, oa_s = _in_proj_with_moba_sample(
        xp.astype(BF16), w["in"], cos_p, sin_p, 512, z_s, k_s, v_s, cache_k[l], cache_v[l], page_table,
        n_db, n_tok)

    on_p, s_p = _retention(z_p, None, _retention_tables(RET_CHUNK), gain, n_b, seq // RET_CHUNK,
                           RET_CHUNK, BF16, n_heads=4)
    oa_p = _moba_prompt(z_p, k_p, v_p, n_b, seq)
    y_p = _merge(xp, z_p, on_p, oa_p, w, 512)

    on_s, s_s = _retention(z_s, state_ret[l], _retention_tables(n_tok), gain, n_db, 1, n_tok, F32,
                           n_heads=H_RET)
    y_s = _merge(xs, z_s, on_s, oa_s, w, tm_s)

    return (
        y_p.reshape(n_b, seq, D_MODEL),
        y_s.reshape(n_db, n_tok, D_MODEL),
        k_p.reshape(1, n_b, seq, H_ATT, HD_ATT),
        v_p.reshape(1, n_b, seq, H_ATT, HD_ATT),
        s_p[None],
        k_s.reshape(1, n_db, n_tok, H_ATT, HD_ATT),
        v_s.reshape(1, n_db, n_tok, H_ATT, HD_ATT),
        s_s[None],
    )
```

```python
import functools
import math

import jax
import jax.numpy as jnp
from jax import lax
from jax.experimental import pallas as pl
from jax.experimental.pallas import tpu as pltpu

F32 = jnp.float32
BF16 = jnp.bfloat16

D_MODEL = 2048
DEPTH = 1
PAST_LEN = 8192
PAGE_SIZE = 128
H_RET = 8
DK_RET = 256
DV_RET = 256
RET_CHUNK = 128
H_ATT = 8
HD_ATT = 128
MOBA_BLOCK = 256
MOBA_TOPK = 3
D_FF = 4 * D_MODEL
RET_QK = H_RET * DK_RET
RET_V = H_RET * DV_RET
ATT_W = H_ATT * HD_ATT
W_IN_COLS = 2 * RET_QK + 2 * RET_V + 3 * ATT_W + 2 * D_MODEL
ROPE_BASE = 10000.0
LN_EPS = 1e-5
GN_EPS = 1e-5
ALPHA = (2 * DEPTH) ** 0.25

COL_QR = 0
COL_KR = RET_QK
COL_VR = 2 * RET_QK
COL_GR = 2 * RET_QK + RET_V
COL_QA = 2 * RET_QK + 2 * RET_V
COL_KA = COL_QA + ATT_W
COL_VA = COL_KA + ATT_W
COL_GBR = COL_VA + ATT_W
COL_GBA = COL_GBR + D_MODEL

NEG_BIG = -1e30
SOFTMAX_C = HD_ATT ** -0.5 * math.log2(math.e)
PAGES_PER_BLOCK = MOBA_BLOCK // PAGE_SIZE
VMEM_LIMIT = 56 * 1024 * 1024


def _params(semantics):
    return pltpu.CompilerParams(dimension_semantics=semantics, vmem_limit_bytes=VMEM_LIMIT)


def _sigmoid(x):
    return 1.0 / (1.0 + jnp.exp(-x))


def _dot(a, b):
    return jnp.dot(a, b, preferred_element_type=F32)


def _dot_nt(a, b):
    return lax.dot_general(a, b, (((1,), (1,)), ((), ())), preferred_element_type=F32)


def _dot_tn(a, b):
    return lax.dot_general(a, b, (((0,), (0,)), ((), ())), preferred_element_type=F32)


def _layer_norm_rows(x, g, b):
    mu = jnp.mean(x, axis=-1, keepdims=True)
    xc = x - mu
    var = jnp.mean(xc * xc, axis=-1, keepdims=True)
    return xc * lax.rsqrt(var + LN_EPS) * g + b


IN_TN = 512
IN_STEP = 1024
_J_ROT_END = COL_VR // IN_STEP
_J_KR = COL_KR // IN_STEP
_J_KA = COL_KA // IN_STEP
_J_VA = COL_VA // IN_STEP
assert ATT_W == IN_STEP and COL_VR % IN_STEP == 0 and COL_QA % IN_STEP == 0


def _in_proj_kernel(*refs):
    _in_proj_body(pl.program_id(1), None, *refs)


def _in_proj_body(j, active, x_ref, w_ref, cos_ref, sin_ref, z_ref, k_ref, v_ref, xb_ref):
    tiles = [slice(t * IN_TN, (t + 1) * IN_TN) for t in range(IN_STEP // IN_TN)]
    when = lambda cond: pl.when(cond if active is None else active & cond)

    if xb_ref is None:
        lhs_ref = x_ref
    else:
        lhs_ref = xb_ref

        @when(j == 0)
        def _():
            xb_ref[...] = x_ref[...].astype(BF16)

    def tile_dot(sl):
        return _dot(lhs_ref[...], w_ref[:, sl])

    @when(j < _J_ROT_END)
    def _():
        cos = cos_ref[...]
        sin = sin_ref[...]
        scale = jnp.where(j >= _J_KR, DK_RET ** -0.5, 1.0).astype(F32)
        half = DK_RET // 2
        for sl in tiles:
            acc = tile_dot(sl)
            for hh in range(IN_TN // DK_RET):
                c0 = hh * DK_RET
                o0 = sl.start + c0
                x1 = acc[:, c0:c0 + half]
                x2 = acc[:, c0 + half:c0 + DK_RET]
                z_ref[:, o0:o0 + half] = ((x1 * cos - x2 * sin) * scale).astype(z_ref.dtype)
                z_ref[:, o0 + half:o0 + DK_RET] = ((x1 * sin + x2 * cos) * scale).astype(z_ref.dtype)

    @when((j >= _J_ROT_END) & (j != _J_KA) & (j != _J_VA))
    def _():
        for sl in tiles:
            z_ref[:, sl] = tile_dot(sl).astype(z_ref.dtype)

    def new_rows(out_ref):
        for sl in tiles:
            acc = tile_dot(sl)
            z_ref[:, sl] = acc.astype(z_ref.dtype)
            out_ref[:, sl] = acc

    when(j == _J_KA)(lambda: new_rows(k_ref))
    when(j == _J_VA)(lambda: new_rows(v_ref))


def _in_proj(x, w_bf, cos_t, sin_t, tm, z_dtype):
    m = x.shape[0]
    n_tab = cos_t.shape[0] // tm
    return pl.pallas_call(
        _in_proj_kernel,
        grid=(m // tm, W_IN_COLS // IN_STEP),
        in_specs=[
            pl.BlockSpec((tm, D_MODEL), lambda i, j: (i, 0), pipeline_mode=pl.Buffered(1)),
            pl.BlockSpec((D_MODEL, IN_STEP), lambda i, j: (0, j)),
            pl.BlockSpec((tm, DK_RET // 2), lambda i, j: (i % n_tab, 0)),
            pl.BlockSpec((tm, DK_RET // 2), lambda i, j: (i % n_tab, 0)),
        ],
        out_specs=[
            pl.BlockSpec((tm, IN_STEP), lambda i, j: (i, j)),
            pl.BlockSpec((tm, ATT_W), lambda i, j: (i, 0)),
            pl.BlockSpec((tm, ATT_W), lambda i, j: (i, 0)),
        ],
        out_shape=[
            jax.ShapeDtypeStruct((m, W_IN_COLS), z_dtype),
            jax.ShapeDtypeStruct((m, ATT_W), F32),
            jax.ShapeDtypeStruct((m, ATT_W), F32),
        ],
        scratch_shapes=[pltpu.VMEM((tm, D_MODEL), BF16)],
        compiler_params=_params(("arbitrary", "arbitrary")),
        name="in_proj",
    )(x, w_bf, cos_t, sin_t)


def _retention_kernel(*refs, c_data, has_init, n_heads):
    if has_init:
        (q_ref, k_ref, v_ref, g_ref, s0_ref, dec_ref, qd_ref, kd_ref, gc_ref, gain_ref,
         on_ref, s_ref, pad_ref) = refs
    else:
        (q_ref, k_ref, v_ref, g_ref, dec_ref, qd_ref, kd_ref, gc_ref, gain_ref,
         on_ref, s_ref, pad_ref) = refs
        s0_ref = None
    c = pl.program_id(2)
    padded = c_data != RET_CHUNK

    @pl.when(c == 0)
    def _():
        if has_init:
            s_ref[...] = s0_ref[...]
        else:
            s_ref[...] = jnp.zeros(s_ref.shape, F32)

    def load(ref, slot):
        if not padded:
            return ref[...]
        pad_ref[slot] = jnp.zeros(pad_ref.shape[1:], F32)
        pad_ref[slot, 0:c_data, :] = ref[...].astype(F32)
        return pad_ref[slot]

    q_all = load(q_ref, 0)
    k_all = load(k_ref, 1)
    v_all = load(v_ref, 2)
    g_all = g_ref[...]

    for hh in range(n_heads):
        lo, hi = hh * DK_RET, (hh + 1) * DK_RET
        q = q_all[:, lo:hi].astype(BF16)
        k = k_all[:, lo:hi]
        v = v_all[:, lo:hi].astype(BF16)
        s_old = s_ref[0, hh]
        scores = _dot_nt(q, k.astype(BF16)) * dec_ref[hh]
        inner = _dot(scores.astype(BF16), v)
        cross = _dot(q, s_old.astype(BF16)) * qd_ref[hh]
        o = inner + cross
        kd = (k.astype(F32) * kd_ref[hh]).astype(BF16)
        s_ref[0, hh] = gc_ref[hh] * s_old + _dot_tn(kd, v)
        if padded:
            o = o[0:c_data]
        mu = jnp.mean(o, axis=-1, keepdims=True)
        oc = o - mu
        var = jnp.mean(oc * oc, axis=-1, keepdims=True)
        o_n = oc * lax.rsqrt(var + GN_EPS) * gain_ref[hh]
        g = g_all[:, lo:hi].astype(F32)
        on_ref[:, lo:hi] = (o_n * (g * _sigmoid(g))).astype(on_ref.dtype)


def _retention(z, s0, tabs, gain, n_batch, n_chunks, c_data, out_dtype, n_heads):
    dec, qd, kd, gc = tabs
    m = z.shape[0]
    wblk = DK_RET * n_heads
    has_init = s0 is not None

    def zspec(col0):
        base = col0 // wblk
        return pl.BlockSpec((c_data, wblk), lambda b, hg, c: (b * n_chunks + c, base + hg))

    tab3 = lambda shape: pl.BlockSpec((n_heads,) + shape, lambda b, hg, c: (hg, 0, 0))
    sspec = pl.BlockSpec((1, n_heads, DK_RET, DV_RET), lambda b, hg, c: (b, hg, 0, 0))
    in_specs = [zspec(COL_QR), zspec(COL_KR), zspec(COL_VR), zspec(COL_GR)]
    args = [z, z, z, z]
    if has_init:
        in_specs.append(sspec)
        args.append(s0)
    in_specs += [tab3((RET_CHUNK, RET_CHUNK)), tab3((RET_CHUNK, 1)), tab3((RET_CHUNK, 1)),
                 tab3((1, 1)), tab3((1, DV_RET))]
    args += [dec, qd, kd, gc, gain]
    return pl.pallas_call(
        functools.partial(_retention_kernel, c_data=c_data, has_init=has_init, n_heads=n_heads),
        grid=(n_batch, H_RET // n_heads, n_chunks),
        in_specs=in_specs,
        out_specs=[
            pl.BlockSpec((c_data, wblk), lambda b, hg, c: (b * n_chunks + c, hg)),
            sspec,
        ],
        out_shape=[
            jax.ShapeDtypeStruct((m, RET_V), out_dtype),
            jax.ShapeDtypeStruct((n_batch, H_RET, DK_RET, DV_RET), F32),
        ],
        scratch_shapes=[pltpu.VMEM((3, RET_CHUNK, wblk), F32)],
        compiler_params=_params(("arbitrary", "arbitrary", "arbitrary")),
        name="retention",
    )(*args)


def _retention_tables(c_data):
    log_gamma = jnp.log1p(-jnp.exp2(-5.0 - jnp.arange(H_RET, dtype=F32)))
    idx = jnp.arange(RET_CHUNK, dtype=F32)
    live = idx < c_data
    diff = idx[:, None] - idx[None, :]
    causal = (diff >= 0) & live[:, None] & live[None, :]
    dec = jnp.where(causal, jnp.exp(log_gamma[:, None, None] * jnp.where(causal, diff, 0.0)), 0.0)
    qd = jnp.exp(log_gamma[:, None] * (idx[None, :] + 1.0))
    kd = jnp.where(live[None, :], jnp.exp(log_gamma[:, None] * (c_data - 1.0 - idx[None, :])), 0.0)
    gc = jnp.exp(log_gamma * c_data)
    return dec, qd[:, :, None], kd[:, :, None], gc[:, None, None]


def _moba_prompt_kernel(q_ref, k_ref, v_ref, o_ref, kb_ref, vb_ref, km_ref):
    nb = k_ref.shape[0] // MOBA_BLOCK
    blk = MOBA_BLOCK

    kb_ref[...] = k_ref[...].astype(BF16)
    vb_ref[...] = v_ref[...].astype(BF16)
    km_ref[...] = jnp.zeros(km_ref.shape, F32)
    for n in range(nb):
        km_ref[n:n + 1, :] = jnp.sum(k_ref[n * blk:(n + 1) * blk, :], axis=0, keepdims=True) * (1.0 / blk)
    kmb = km_ref[...].astype(BF16)

    lane = lax.broadcasted_iota(jnp.int32, (blk, 128), 1)
    row = lax.broadcasted_iota(jnp.int32, (blk, blk), 0)
    col = lax.broadcasted_iota(jnp.int32, (blk, blk), 1)
    causal_bias = jnp.where(col <= row, 0.0, NEG_BIG).astype(F32)

    for i in range(nb):
        q = q_ref[i * blk:(i + 1) * blk, :].astype(BF16)
        pieces = []
        if i > 0:
            s_el = jnp.where(lane < i, _dot_nt(q, kmb), -jnp.inf)
            cnt = jnp.zeros(s_el.shape, jnp.int32)
            for n in range(i):
                cn = s_el[:, n:n + 1]
                ahead = (cn > s_el) | ((cn == s_el) & (lane > n))
                cnt = cnt + ahead.astype(jnp.int32)
            bias = jnp.where((lane < i) & (cnt < MOBA_TOPK), 0.0, NEG_BIG).astype(F32)
            pieces = [jnp.broadcast_to(bias[:, n:n + 1], (blk, blk)) for n in range(i)]
        nk = (i + 1) * blk
        s = _dot_nt(q, kb_ref[0:nk, :]) + jnp.concatenate(pieces + [causal_bias], axis=1)
        m = jnp.max(s, axis=-1, keepdims=True)
        p = jnp.exp2((s - m) * SOFTMAX_C)
        l = jnp.sum(p, axis=-1, keepdims=True)
        o = _dot(p.astype(BF16), vb_ref[0:nk, :]) / l
        o_ref[i * blk:(i + 1) * blk, :] = o.astype(o_ref.dtype)


def _moba_prompt(z, k_new, v_new, n_batch, seq):
    m = z.shape[0]
    qcol = COL_QA // HD_ATT
    head_rows = lambda col0: pl.BlockSpec((seq, HD_ATT), lambda b, h: (b, col0 + h))
    return pl.pallas_call(
        _moba_prompt_kernel,
        grid=(n_batch, H_ATT),
        in_specs=[head_rows(qcol), head_rows(0), head_rows(0)],
        out_specs=head_rows(0),
        out_shape=jax.ShapeDtypeStruct((m, ATT_W), z.dtype),
        scratch_shapes=[
            pltpu.VMEM((seq, HD_ATT), BF16),
            pltpu.VMEM((seq, HD_ATT), BF16),
            pltpu.VMEM((128, HD_ATT), F32),
        ],
        compiler_params=_params(("arbitrary", "arbitrary")),
        name="moba_prompt",
    )(z, k_new, v_new)


SMP_G = 4
SMP_SUB = 512


def _moba_sample_body(s_id, n_steps, q_ref, kn_ref, vn_ref, hb_ref, ck, cv,
                      o_ref, m_all, l_all, s_all, o_sc, kpad, vpad, *, n_tok, n_past):
    rows = H_ATT * n_tok
    tok_shift = n_tok.bit_length() - 1
    assert n_tok == 1 << tok_shift and H_ATT & (H_ATT - 1) == 0 and rows <= 128

    @pl.when(s_id == 0)
    def _():
        m_all[...] = jnp.zeros(m_all.shape, F32)
        l_all[...] = jnp.zeros(l_all.shape, F32)
        s_all[...] = jnp.zeros(s_all.shape, F32)

    def by_head(ref):
        x = ref[...]
        return jnp.concatenate([x[:, h * HD_ATT:(h + 1) * HD_ATT] for h in range(H_ATT)], axis=0)

    qb = by_head(q_ref).astype(BF16)
    qr = qb.astype(F32)
    hb = hb_ref[...]
    lane = lax.broadcasted_iota(jnp.int32, (rows, 128), 1)
    subs = [slice(t * SMP_SUB, (t + 1) * SMP_SUB) for t in range(PAGE_SIZE * H_ATT // SMP_SUB)]
    ppb = PAGES_PER_BLOCK * len(subs)
    ppb_shift = ppb.bit_length() - 1
    n_parts = n_past * ppb
    assert ppb == 1 << ppb_shift and n_parts <= 128

    pieces = []
    for g in range(SMP_G):
        for pg in range(PAGES_PER_BLOCK):
            for t, rs in enumerate(subs):
                pieces.append((g * ppb + pg * len(subs) + t, g, PAGES_PER_BLOCK * g + pg, rs))
    first = s_id * (SMP_G * ppb)

    ksum = [jnp.zeros((H_ATT, HD_ATT), F32) for _ in range(SMP_G)]
    scores = []
    for _, g, pi, rs in pieces:
        kt = ck[pi][rs, :]
        ksum[g] = ksum[g] + jnp.sum(kt.reshape(SMP_SUB // H_ATT, H_ATT, HD_ATT), axis=0)
        scores.append(_dot_nt(qb, kt.astype(BF16)))
    probs, stats = [], []
    for s in scores:
        s = s + hb
        m_t = jnp.max(s, axis=-1, keepdims=True)
        p = jnp.exp2((s - m_t) * SOFTMAX_C)
        probs.append(p.astype(BF16))
        stats.append((m_t, jnp.sum(p, axis=-1, keepdims=True)))
    for (local, _, pi, rs), p in zip(pieces, probs):
        o_sc[first + local] = _dot(p, cv[pi][rs, :].astype(BF16))

    m_new, l_new, s_new = m_all[...], l_all[...], s_all[...]
    for g in range(SMP_G):
        kmean = (ksum[g] * (1.0 / MOBA_BLOCK)).astype(BF16).astype(F32)
        kexp = jnp.concatenate(
            [jnp.broadcast_to(kmean[h:h + 1, :], (n_tok, HD_ATT)) for h in range(H_ATT)], axis=0)
        sc = jnp.sum(qr * kexp, axis=-1, keepdims=True)
        for (local, gg, _, _), (m_t, l_t) in zip(pieces, stats):
            if gg == g:
                here = lane == first + local
                m_new = jnp.where(here, m_t, m_new)
                l_new = jnp.where(here, l_t, l_new)
                s_new = jnp.where(here, sc, s_new)
    m_all[...] = m_new
    l_all[...] = l_new
    s_all[...] = s_new

    @pl.when(s_id == n_steps - 1)
    def _():
        past = lane < n_parts
        lane_blk = lane >> ppb_shift
        sc_all = jnp.where(past, s_all[...], -jnp.inf)
        cnt = jnp.zeros(sc_all.shape, jnp.int32)
        for nn in range(n_past):
            col = sc_all[:, nn * ppb:nn * ppb + 1]
            ahead = (col > sc_all) | ((col == sc_all) & (lane_blk > nn))
            cnt = cnt + ahead.astype(jnp.int32)
        sel = past & (cnt < MOBA_TOPK)
        m_sel = jnp.where(sel, m_all[...], NEG_BIG)

        kpad[...] = jnp.zeros(kpad.shape, F32)
        vpad[...] = jnp.zeros(vpad.shape, F32)
        kpad[0:rows, :] = by_head(kn_ref)
        vpad[0:rows, :] = by_head(vn_ref)
        r_id = lax.broadcasted_iota(jnp.int32, (rows, 128), 0)
        own_ok = ((lane < rows) & ((lane >> tok_shift) == (r_id >> tok_shift))
                  & ((lane & (n_tok - 1)) <= (r_id & (n_tok - 1))))
        s_own = jnp.where(own_ok, _dot_nt(qb, kpad[...].astype(BF16)), NEG_BIG)
        big = jnp.maximum(jnp.max(m_sel, axis=-1, keepdims=True), jnp.max(s_own, axis=-1, keepdims=True))
        w = jnp.where(sel, jnp.exp2((m_sel - big) * SOFTMAX_C), 0.0)
        p_own = jnp.exp2((s_own - big) * SOFTMAX_C)
        den = jnp.sum(w * l_all[...], axis=-1, keepdims=True) + jnp.sum(p_own, axis=-1, keepdims=True)
        num = _dot(p_own.astype(BF16), vpad[...].astype(BF16))
        for pp in range(n_parts):
            num = num + w[:, pp:pp + 1] * o_sc[pp]
        out = num / den
        for h in range(H_ATT):
            o_ref[:, h * HD_ATT:(h + 1) * HD_ATT] = out[h * n_tok:(h + 1) * n_tok, :].astype(o_ref.dtype)


def _proj_moba_kernel(pt_ref, x_ref, w_ref, cos_ref, sin_ref, q_ref, kn_ref, vn_ref, hb_ref, *rest,
                      n_proj, n_row, smp_steps, n_tok, n_past):
    n_pg = SMP_G * PAGES_PER_BLOCK
    ck = rest[:n_pg]
    cv = rest[n_pg:2 * n_pg]
    z_ref, k_ref, v_ref, o_ref, m_all, l_all, s_all, o_sc, kpad, vpad = rest[2 * n_pg:]
    t = pl.program_id(0)
    _in_proj_body(jnp.minimum(t, n_proj - 1) // n_row, t < n_proj,
                  x_ref, w_ref, cos_ref, sin_ref, z_ref, k_ref, v_ref, None)
    _moba_sample_body(t % smp_steps, smp_steps, q_ref, kn_ref, vn_ref, hb_ref, ck, cv,
                      o_ref, m_all, l_all, s_all, o_sc, kpad, vpad, n_tok=n_tok, n_past=n_past)


def _in_proj_with_moba_sample(x, w_bf, cos_t, sin_t, tm, z_s, k_s, v_s, cache_k, cache_v, page_table,
                              n_batch, n_tok):
    assert x.dtype == BF16
    m = x.shape[0]
    n_col = W_IN_COLS // IN_STEP
    n_row = m // tm
    n_proj = n_row * n_col
    n_tab = cos_t.shape[0] // tm
    n_pages = page_table.shape[1]
    n_past = n_pages // PAGES_PER_BLOCK
    n_pg = SMP_G * PAGES_PER_BLOCK
    smp_steps = n_past // SMP_G
    n_grid = n_batch * smp_steps
    assert n_grid >= n_proj
    pt = page_table.reshape(-1)
    ck = cache_k.reshape(cache_k.shape[0], PAGE_SIZE * H_ATT, HD_ATT)
    cv = cache_v.reshape(cache_v.shape[0], PAGE_SIZE * H_ATT, HD_ATT)

    pcol = lambda t: jnp.minimum(t, n_proj - 1) // n_row
    prow = lambda t: jnp.minimum(t, n_proj - 1) % n_row

    def new_rows_spec(j_write):
        def index(t, pt_ref):
            j = pcol(t)
            return (jnp.where(j < j_write, 0, jnp.where(j == j_write, prow(t), n_row - 1)), 0)
        return pl.BlockSpec((tm, ATT_W), index)

    def page_spec(jj):
        return pl.BlockSpec(
            (None, PAGE_SIZE * H_ATT, HD_ATT),
            lambda t, pt_ref: (pt_ref[(t // smp_steps) * n_pages + (t % smp_steps) * n_pg + jj], 0, 0))

    srow = lambda col: pl.BlockSpec((n_tok, ATT_W), lambda t, pt_ref: (t // smp_steps, col))
    rows = H_ATT * n_tok
    pg_rows = PAGE_SIZE * H_ATT
    row_head = jnp.arange(rows, dtype=jnp.int32)[:, None] // n_tok
    col_head = jnp.arange(SMP_SUB, dtype=jnp.int32)[None, :] % H_ATT
    head_bias = jnp.where(row_head == col_head, 0.0, NEG_BIG).astype(F32)
    n_parts = n_past * pg_rows // SMP_SUB * PAGES_PER_BLOCK
    grid_spec = pltpu.PrefetchScalarGridSpec(
        num_scalar_prefetch=1,
        grid=(n_grid,),
        in_specs=[
            pl.BlockSpec((tm, D_MODEL), lambda t, pt_ref: (prow(t), 0)),
            pl.BlockSpec((D_MODEL, IN_STEP), lambda t, pt_ref: (0, pcol(t))),
            pl.BlockSpec((tm, DK_RET // 2), lambda t, pt_ref: (prow(t) % n_tab, 0)),
            pl.BlockSpec((tm, DK_RET // 2), lambda t, pt_ref: (prow(t) % n_tab, 0)),
            srow(COL_QA // ATT_W), srow(0), srow(0),
            pl.BlockSpec((rows, SMP_SUB), lambda t, pt_ref: (0, 0)),
        ] + [page_spec(jj) for jj in range(n_pg)] + [page_spec(jj) for jj in range(n_pg)],
        out_specs=[
            pl.BlockSpec((tm, IN_STEP), lambda t, pt_ref: (prow(t), pcol(t))),
            new_rows_spec(_J_KA),
            new_rows_spec(_J_VA),
            srow(0),
        ],
        scratch_shapes=[
            pltpu.VMEM((rows, 128), F32),
            pltpu.VMEM((rows, 128), F32),
            pltpu.VMEM((rows, 128), F32),
            pltpu.VMEM((n_parts, rows, HD_ATT), F32),
            pltpu.VMEM((128, HD_ATT), F32),
            pltpu.VMEM((128, HD_ATT), F32),
        ],
    )
    return pl.pallas_call(
        functools.partial(_proj_moba_kernel, n_proj=n_proj, n_row=n_row, smp_steps=smp_steps,
                          n_tok=n_tok, n_past=n_past),
        grid_spec=grid_spec,
        out_shape=[
            jax.ShapeDtypeStruct((m, W_IN_COLS), BF16),
            jax.ShapeDtypeStruct((m, ATT_W), F32),
            jax.ShapeDtypeStruct((m, ATT_W), F32),
            jax.ShapeDtypeStruct((z_s.shape[0], ATT_W), z_s.dtype),
        ],
        compiler_params=_params(("arbitrary",)),
        name="proj_moba",
    )(pt, x, w_bf, cos_t, sin_t, z_s, k_s, v_s, head_bias, *([ck] * n_pg), *([cv] * n_pg))


MIX_TN = 1024
DOT_TN = 512
MIX_OUT_TM = 256


def _mix_out_kernel(on_ref, oa_ref, wr_ref, wa_ref, gr0_ref, gr1_ref, ga0_ref, ga1_ref, wo_ref, x_ref,
                    g_ref, b_ref, h_ref, hb_ref, mix_ref):
    on = on_ref[...].astype(BF16)
    oa = oa_ref[...].astype(BF16)
    gates_r = (gr0_ref, gr1_ref)
    gates_a = (ga0_ref, ga1_ref)
    per_gate = MIX_TN // DOT_TN
    for t in range(D_MODEL // DOT_TN):
        sl = slice(t * DOT_TN, (t + 1) * DOT_TN)
        gl = slice((t % per_gate) * DOT_TN, (t % per_gate + 1) * DOT_TN)
        ret = _dot(on, wr_ref[:, sl])
        att = _dot(oa, wa_ref[:, sl])
        mix = (_sigmoid(gates_r[t // per_gate][:, gl].astype(F32)) * ret
               + _sigmoid(gates_a[t // per_gate][:, gl].astype(F32)) * att)
        mix_ref[:, sl] = mix.astype(BF16)
    h = _layer_norm_rows(ALPHA * x_ref[...] + _dot(mix_ref[...], wo_ref[...]), g_ref[...], b_ref[...])
    h_ref[...] = h
    hb_ref[...] = h.astype(BF16)


def _mix_out(on, oa, z, w_ret, w_att, w_out, x, g, b, tm):
    m = x.shape[0]
    assert D_MODEL == 2 * MIX_TN
    jr = COL_GBR // MIX_TN
    ja = COL_GBA // MIX_TN
    rowspec = pl.BlockSpec((tm, D_MODEL), lambda i: (i, 0))
    vec = pl.BlockSpec((1, D_MODEL), lambda i: (0, 0))
    whole = lambda a: pl.BlockSpec(a.shape, lambda i: (0, 0), pipeline_mode=pl.Buffered(1))
    gate = lambda c: pl.BlockSpec((tm, MIX_TN), lambda i: (i, c))
    return pl.pallas_call(
        _mix_out_kernel,
        grid=(m // tm,),
        in_specs=[rowspec, pl.BlockSpec((tm, ATT_W), lambda i: (i, 0)), whole(w_ret), whole(w_att),
                  gate(jr), gate(jr + 1), gate(ja), gate(ja + 1), whole(w_out), rowspec, vec, vec],
        out_specs=[rowspec, rowspec],
        out_shape=[jax.ShapeDtypeStruct((m, D_MODEL), F32), jax.ShapeDtypeStruct((m, D_MODEL), BF16)],
        scratch_shapes=[pltpu.VMEM((tm, D_MODEL), BF16)],
        compiler_params=_params(("arbitrary",)),
        name="mix_out",
    )(on, oa, w_ret, w_att, z, z, z, z, w_out, x, g, b)


MLP_TF = 1024


def _mlp_ln_kernel(h_ref, hb_ref, wu_ref, wd_ref, g_ref, b_ref, y_ref, acc_ref):
    f = pl.program_id(1)

    @pl.when(f == 0)
    def _():
        acc_ref[...] = ALPHA * h_ref[...]

    hb = hb_ref[...]
    pieces = []
    for c in range(MLP_TF // DOT_TN):
        a = jnp.maximum(_dot(hb, wu_ref[:, c * DOT_TN:(c + 1) * DOT_TN]), 0.0)
        pieces.append((a * a).astype(BF16))
    act = jnp.concatenate(pieces, axis=1)
    for n in range(D_MODEL // DOT_TN):
        sl = slice(n * DOT_TN, (n + 1) * DOT_TN)
        acc_ref[:, sl] += _dot(act, wd_ref[:, sl])

    @pl.when(f == pl.num_programs(1) - 1)
    def _():
        y_ref[...] = _layer_norm_rows(acc_ref[...], g_ref[...], b_ref[...])


def _mlp_ln(h, hb, w_up, w_down, g, b, tm):
    m = h.shape[0]
    rowspec = pl.BlockSpec((tm, D_MODEL), lambda i, f: (i, 0))
    vec = pl.BlockSpec((1, D_MODEL), lambda i, f: (0, 0))
    return pl.pallas_call(
        _mlp_ln_kernel,
        grid=(m // tm, D_FF // MLP_TF),
        in_specs=[
            rowspec, rowspec,
            pl.BlockSpec((D_MODEL, MLP_TF), lambda i, f: (0, f)),
            pl.BlockSpec((MLP_TF, D_MODEL), lambda i, f: (f, 0)),
            vec, vec,
        ],
        out_specs=rowspec,
        out_shape=jax.ShapeDtypeStruct((m, D_MODEL), F32),
        scratch_shapes=[pltpu.VMEM((tm, D_MODEL), F32)],
        compiler_params=_params(("arbitrary", "arbitrary")),
        name="mlp_ln",
    )(h, hb, w_up, w_down, g, b)


def _rotary_tables(pos):
    half = DK_RET // 2
    inv = ROPE_BASE ** (-jnp.arange(half, dtype=F32) / half)
    ang = pos.astype(F32)[:, None] * inv[None, :]
    return jnp.cos(ang), jnp.sin(ang)


def _merge(x, z, on, oa, w, tm):
    h, hb = _mix_out(on, oa, z, w["ret"], w["att"], w["out"], x, w["ln1_g"], w["ln1_b"], MIX_OUT_TM)
    return _mlp_ln(h, hb, w["up"], w["down"], w["ln2_g"], w["ln2_b"], tm)


def kernel(x_prompt, x_sample, cache_k, cache_v, state_ret, page_table, w_in, ret_gn_gain, w_ret_br,
           w_att_br, w_out, ln1_g, ln1_b, w_up, w_down, ln2_g, ln2_b):
    n_b, seq, _ = x_prompt.shape
    n_db, n_tok, _ = x_sample.shape
    assert w_in.shape[0] == DEPTH == 1
    l = 0
    w = {
        "in": w_in[l].astype(BF16), "ret": w_ret_br[l].astype(BF16), "att": w_att_br[l].astype(BF16),
        "out": w_out[l].astype(BF16), "up": w_up[l].astype(BF16), "down": w_down[l].astype(BF16),
        "ln1_g": ln1_g[l][None, :], "ln1_b": ln1_b[l][None, :],
        "ln2_g": ln2_g[l][None, :], "ln2_b": ln2_b[l][None, :],
    }
    gain = ret_gn_gain[l][:, None, :]

    xs = x_sample.reshape(n_db * n_tok, D_MODEL)
    pos_s = PAST_LEN + jnp.arange(n_tok, dtype=jnp.int32)
    cos_s, sin_s = _rotary_tables(jnp.tile(pos_s, n_db))
    tm_s = n_db * n_tok
    z_s, k_s, v_s = _in_proj(xs, w["in"], cos_s, sin_s, tm_s, F32)
    xp = x_prompt.reshape(n_b * seq, D_MODEL)
    cos_p, sin_p = _rotary_tables(jnp.arange(seq, dtype=jnp.int32))
    z_p, k_p, v_p, oa_s = _in_proj_with_moba_sample(
        xp.astype(BF16), w["in"], cos_p, sin_p, 512, z_s, k_s, v_s, cache_k[l], cache_v[l], page_table,
        n_db, n_tok)

    on_p, s_p = _retention(z_p, None, _retention_tables(RET_CHUNK), gain, n_b, seq // RET_CHUNK,
                           RET_CHUNK, BF16, n_heads=4)
    oa_p = _moba_prompt(z_p, k_p, v_p, n_b, seq)
    y_p = _merge(xp, z_p, on_p, oa_p, w, 512)

    on_s, s_s = _retention(z_s, state_ret[l], _retention_tables(n_tok), gain, n_db, 1, n_tok, F32,
                           n_heads=H_RET)
    y_s = _merge(xs, z_s, on_s, oa_s, w, tm_s)

    return (
        y_p.reshape(n_b, seq, D_MODEL),
        y_s.reshape(n_db, n_tok, D_MODEL),
        k_p.reshape(1, n_b, seq, H_ATT, HD_ATT),
        v_p.reshape(1, n_b, seq, H_ATT, HD_ATT),
        s_p[None],
        k_s.reshape(1, n_db, n_tok, H_ATT, HD_ATT),
        v_s.reshape(1, n_db, n_tok, H_ATT, HD_ATT),
        s_s[None],
    )
```
